```python
import math
import jax
import jax.numpy as jnp
from jax import lax
import numpy as np

D_MODEL = 1024
BATCH = 16
SEQ = 2048
DEPTH = 2

CTX_LEN = 256
GRID_W = 64
EPS = 1e-6
NEG_INF = -1e30

GDN_HEADS = 4
GDN_DK = 128
GDN_DV = 128
GDN_QKV = GDN_HEADS * (2 * GDN_DK + GDN_DV)
GDN_WIDTH = GDN_HEADS * GDN_DV
SHORT_CONV = 5
CHUNK = 64

ATT_HEADS = 8
ATT_KV_HEADS = 2
ATT_GROUP = ATT_HEADS // ATT_KV_HEADS
ATT_HD = 64
ATT_WIDTH = ATT_HEADS * ATT_HD
ATT_KV_WIDTH = ATT_KV_HEADS * ATT_HD
WINDOW = 128
ATT_BLOCK = 128
ROPE_BASE = 10000.0
AXIS_DIM = ATT_HD // 2

STATE_SIZES = (GDN_QKV, 2 * GDN_HEADS, 2 * GDN_HEADS, ATT_KV_WIDTH, ATT_KV_WIDTH)
REST_SIZES = (GDN_WIDTH, ATT_WIDTH, ATT_WIDTH, D_MODEL, D_MODEL)
N_STATE = GDN_QKV + 4 * GDN_HEADS + 2 * ATT_KV_WIDTH
N_IN = N_STATE + GDN_WIDTH + 2 * ATT_WIDTH + 2 * D_MODEL

kernel_name = 'hybrid_gdn_swa_prefix_dit'


def _split_cols(p, sizes):
    return jnp.split(p, [int(i) for i in np.cumsum(sizes)[:-1]], axis=-1)


def _rmsnorm(x, w):
    xf = x.astype(jnp.float32)
    y = xf * lax.rsqrt(jnp.mean(xf * xf, axis=-1, keepdims=True) + EPS)
    return (y * w.astype(jnp.float32)).astype(x.dtype)


def _l2norm(x):
    xf = x.astype(jnp.float32)
    return (xf * lax.rsqrt(jnp.sum(xf * xf, axis=-1, keepdims=True) + EPS)).astype(x.dtype)


def _short_conv(x, w):
    return lax.conv_general_dilated(
        x, w[:, None, :].astype(x.dtype), window_strides=(1,),
        padding=[(SHORT_CONV // 2, SHORT_CONV // 2)],
        dimension_numbers=('NWC', 'WIO', 'NWC'), feature_group_count=x.shape[-1])


def _axial_rope(x, pos_row, pos_col):
    inv = ROPE_BASE ** (-jnp.arange(0, AXIS_DIM, 2, dtype=jnp.float32) / AXIS_DIM)
    xf = x.astype(jnp.float32)

    def rot(xa, pos):
        ang = pos[:, None] * inv[None, :]
        cos, sin = jnp.cos(ang)[None, :, None, :], jnp.sin(ang)[None, :, None, :]
        x1, x2 = jnp.split(xa, 2, axis=-1)
        return jnp.concatenate([x1 * cos - x2 * sin, x2 * cos + x1 * sin], axis=-1)

    xr, xc = jnp.split(xf, 2, axis=-1)
    return jnp.concatenate([rot(xr, pos_row), rot(xc, pos_col)], axis=-1).astype(x.dtype)


def _delta_state_update(s, u_n, w_n, kd_n, gl_n):
    v_new = u_n - jnp.einsum('bhck,bhkv->bhcv', w_n, s)
    s_new = s * jnp.exp(gl_n)[..., None, None] + jnp.einsum('bhck,bhcv->bhkv', kd_n, v_new)
    return s_new, v_new


def _gdn_chunked(q, k, v, g, beta, s0, with_output):
    b, l, h, _ = q.shape
    dv = v.shape[-1]
    n = l // CHUNK

    def to_chunks(t):
        t = t.reshape((b, n, CHUNK, h) + t.shape[3:])
        return jnp.moveaxis(t, (1, 3), (0, 2))

    qc, kc, vc = to_chunks(q), to_chunks(k), to_chunks(v)
    gc = jnp.cumsum(to_chunks(g), axis=-1)
    bc = to_chunks(beta)
    incl = jnp.tril(jnp.ones((CHUNK, CHUNK), bool))
    strict = jnp.tril(jnp.ones((CHUNK, CHUNK), bool), -1)
    diff = gc[..., :, None] - gc[..., None, :]
    decay = jnp.where(incl, jnp.exp(jnp.where(incl, diff, 0.0)), 0.0)
    kb = kc * bc[..., None]
    a_mat = jnp.where(strict, jnp.einsum('nbhik,nbhjk->nbhij', kb, kc) * decay, 0.0)
    eye = jnp.broadcast_to(jnp.eye(CHUNK, dtype=a_mat.dtype), a_mat.shape)
    t_mat = lax.linalg.triangular_solve(a_mat, eye, left_side=True, lower=True, unit_diagonal=True)
    u = jnp.einsum('nbhij,nbhjv->nbhiv', t_mat, vc * bc[..., None])
    w = jnp.einsum('nbhij,nbhjk->nbhik', t_mat, kb * jnp.exp(gc)[..., None])
    g_last = gc[..., -1]
    k_dec = kc * jnp.exp(g_last[..., None] - gc)[..., None]

    if with_output:
        qk = jnp.where(incl, jnp.einsum('nbhik,nbhjk->nbhij', qc, kc) * decay, 0.0)
        q_dec = qc * jnp.exp(gc)[..., None]

        def step(s, xs):
            u_n, w_n, kd_n, gl_n, qk_n, qd_n = xs
            s_new, v_new = _delta_state_update(s, u_n, w_n, kd_n, gl_n)
            o = jnp.einsum('bhck,bhkv->bhcv', qd_n, s) + jnp.einsum('bhij,bhjv->bhiv', qk_n, v_new)
            return s_new, o

        s_fin, o = lax.scan(step, s0, (u, w, k_dec, g_last, qk, q_dec))
        o = jnp.moveaxis(o, (0, 2), (1, 3)).reshape(b, l, h, dv)
        return o, s_fin

    def step_state(s, xs):
        u_n, w_n, kd_n, gl_n = xs
        s_new, _ = _delta_state_update(s, u_n, w_n, kd_n, gl_n)
        return s_new, None

    s_fin, _ = lax.scan(step_state, s0, (u, w, k_dec, g_last))
    return None, s_fin


def _gdn_inputs(qkv_raw, beta_raw, decay_raw, conv_w, a_log, dt_bias):
    b, l, _ = qkv_raw.shape
    qkv = jax.nn.silu(_short_conv(qkv_raw, conv_w))
    q, k, v = jnp.split(qkv, [GDN_HEADS * GDN_DK, 2 * GDN_HEADS * GDN_DK], axis=-1)
    q = _l2norm(q.reshape(b, l, GDN_HEADS, GDN_DK)) * (GDN_DK ** -0.5)
    k = _l2norm(k.reshape(b, l, GDN_HEADS, GDN_DK))
    v = v.reshape(b, l, GDN_HEADS, GDN_DV)
    beta = jax.nn.sigmoid(beta_raw.astype(jnp.float32)).reshape(b, l, 2, GDN_HEADS)
    g = -jnp.exp(a_log.astype(jnp.float32)) * jax.nn.softplus(
        decay_raw.astype(jnp.float32).reshape(b, l, 2, GDN_HEADS) + dt_bias.astype(jnp.float32))
    return q, k, v, g, beta


def _gdn_bidir(q, k, v, g, beta, s0_f, s0_b, with_output):
    q, k, v = q.astype(jnp.float32), k.astype(jnp.float32), v.astype(jnp.float32)
    o_f, s_f = _gdn_chunked(q, k, v, g[:, :, 0], beta[:, :, 0], s0_f, with_output)
    rev = lambda t: jnp.flip(t, axis=1)
    o_b, s_b = _gdn_chunked(rev(q), rev(k), rev(v), rev(g[:, :, 1]), rev(beta[:, :, 1]), s0_b, with_output)
    o = o_f + rev(o_b) if with_output else None
    return o, s_f, s_b


def _gdn_output(o, z, gdn_norm_w):
    b, l = o.shape[:2]
    o = _rmsnorm(o, gdn_norm_w).reshape(b, l, GDN_WIDTH).astype(z.dtype)
    return o * jax.nn.silu(z)


def _latent_attention(q, k, v, kc, vc, sink):
    b, l = q.shape[:2]
    nb = l // ATT_BLOCK
    scale = ATT_HD ** -0.5
    qb = q.reshape(b, nb, ATT_BLOCK, ATT_KV_HEADS, ATT_GROUP, ATT_HD)

    def windows(t):
        tp = jnp.pad(t, ((0, 0), (ATT_BLOCK, ATT_BLOCK), (0, 0), (0, 0)))
        tp = tp.reshape(b, nb + 2, ATT_BLOCK, ATT_KV_HEADS, ATT_HD)
        return jnp.concatenate([tp[:, :-2], tp[:, 1:-1], tp[:, 2:]], axis=2)

    kw, vw = windows(k), windows(v)
    s_loc = jnp.einsum('bnqhgd,bnkhd->bhgnqk', qb, kw).astype(jnp.float32) * scale
    qi = jnp.arange(ATT_BLOCK)[:, None] + ATT_BLOCK
    kj = jnp.arange(3 * ATT_BLOCK)[None, :]
    kabs = jnp.arange(nb)[:, None, None] * ATT_BLOCK + kj[None] - ATT_BLOCK
    valid = (jnp.abs(qi - kj) <= WINDOW)[None] & (kabs >= 0) & (kabs < l)
    s_loc = jnp.where(valid, s_loc, NEG_INF)
    s_ctx = jnp.einsum('bnqhgd,bkhd->bhgnqk', qb, kc).astype(jnp.float32) * scale
    sink_l = jnp.broadcast_to(
        sink.astype(jnp.float32).reshape(ATT_KV_HEADS, ATT_GROUP)[None, :, :, None, None, None],
        s_loc.shape[:-1] + (1,))
    p = jax.nn.softmax(jnp.concatenate([s_loc, s_ctx, sink_l], axis=-1), axis=-1).astype(v.dtype)
    nloc = 3 * ATT_BLOCK
    nctx = kc.shape[1]
    o = (jnp.einsum('bhgnqk,bnkhd->bnqhgd', p[..., :nloc], vw)
         + jnp.einsum('bhgnqk,bkhd->bnqhgd', p[..., nloc:nloc + nctx], vc))
    return o.reshape(b, l, ATT_WIDTH)


def _context_attention(q, kc, vc, sink):
    b, lc = q.shape[:2]
    s = jnp.einsum('bqhgd,bkhd->bhgqk', q, kc).astype(jnp.float32) * (ATT_HD ** -0.5)
    sink_c = jnp.broadcast_to(
        sink.astype(jnp.float32).reshape(ATT_KV_HEADS, ATT_GROUP)[None, :, :, None, None],
        s.shape[:-1] + (1,))
    p = jax.nn.softmax(jnp.concatenate([s, sink_c], axis=-1), axis=-1)[..., :-1].astype(vc.dtype)
    o = jnp.einsum('bhgqk,bkhd->bqhgd', p, vc)
    return o.reshape(b, lc, ATT_WIDTH)


def _merge(ya, yb, ga, gb, w_proj_a, w_proj_b, w_out):
    y = jax.nn.sigmoid(ga) * (ya @ w_proj_a) + jax.nn.sigmoid(gb) * (yb @ w_proj_b)
    return y @ w_out


def _layer(x, ctx, c, c_ctx, pos_row, pos_col, norm_w, w_mod, b_mod, w_in, conv_w, a_log,
           dt_bias, gdn_norm_w, q_norm_w, k_norm_w, sink, w_proj_a, w_proj_b, w_out, update_ctx):
    b, l, _ = x.shape
    lc = ctx.shape[1]
    shift, scale, gate = jnp.split(jax.nn.silu(c) @ w_mod + b_mod, 3, axis=-1)
    shift_c, scale_c, gate_c = jnp.split(jax.nn.silu(c_ctx) @ w_mod + b_mod, 3, axis=-1)
    h = _rmsnorm(x, norm_w) * (1.0 + scale[:, None]) + shift[:, None]
    hc = _rmsnorm(ctx, norm_w) * (1.0 + scale_c) + shift_c
    w_state, w_rest = w_in[:, :N_STATE], w_in[:, N_STATE:]

    qkv_c, beta_c, decay_c, kb_c, vb_c = _split_cols(hc @ w_state, STATE_SIZES)
    qa_c, ka_c, va_c, g_c, bt_c = _gdn_inputs(qkv_c, beta_c, decay_c, conv_w, a_log, dt_bias)
    s0 = jnp.zeros((b, GDN_HEADS, GDN_DK, GDN_DV), jnp.float32)
    oa_c, s_f, s_b = _gdn_bidir(qa_c, ka_c, va_c, g_c, bt_c, s0, s0, update_ctx)
    kb_c = _rmsnorm(kb_c.reshape(b, lc, ATT_KV_HEADS, ATT_HD), k_norm_w)
    vb_c = vb_c.reshape(b, lc, ATT_KV_HEADS, ATT_HD)

    qkv, beta_r, decay_r, kb, vb = _split_cols(h @ w_state, STATE_SIZES)
    za, qb, zb, ga, gb = _split_cols(h @ w_rest, REST_SIZES)
    qa, ka, va, g, bt = _gdn_inputs(qkv, beta_r, decay_r, conv_w, a_log, dt_bias)
    oa, _, _ = _gdn_bidir(qa, ka, va, g, bt, s_f, s_b, True)
    ya = _gdn_output(oa, za, gdn_norm_w)
    qb = _axial_rope(_rmsnorm(qb.reshape(b, l, ATT_HEADS, ATT_HD), q_norm_w), pos_row, pos_col)
    kb = _axial_rope(_rmsnorm(kb.reshape(b, l, ATT_KV_HEADS, ATT_HD), k_norm_w), pos_row, pos_col)
    vb = vb.reshape(b, l, ATT_KV_HEADS, ATT_HD)
    ob = _latent_attention(qb.reshape(b, l, ATT_KV_HEADS, ATT_GROUP, ATT_HD), kb, vb, kb_c, vb_c, sink)
    yb = ob * jax.nn.silu(zb)
    x = x + gate[:, None] * _merge(ya, yb, ga, gb, w_proj_a, w_proj_b, w_out)

    if update_ctx:
        za_c, qb_c, zb_c, ga_c, gb_c = _split_cols(hc @ w_rest, REST_SIZES)
        ya_c = _gdn_output(oa_c, za_c, gdn_norm_w)
        qb_c = _rmsnorm(qb_c.reshape(b, lc, ATT_HEADS, ATT_HD), q_norm_w)
        ob_c = _context_attention(qb_c.reshape(b, lc, ATT_KV_HEADS, ATT_GROUP, ATT_HD), kb_c, vb_c, sink)
        yb_c = ob_c * jax.nn.silu(zb_c)
        ctx = ctx + gate_c * _merge(ya_c, yb_c, ga_c, gb_c, w_proj_a, w_proj_b, w_out)
    return x, ctx


def setup_inputs(seed: int = 0) -> dict:
    key = jax.random.key(seed)
    ks = jax.random.split(key, 18)
    f32 = jnp.float32

    def nrm(k, shape, s):
        return jax.random.normal(k, shape, f32) * s

    x = nrm(ks[0], (BATCH, SEQ, D_MODEL), 1.0)
    c = nrm(ks[1], (BATCH, D_MODEL), 1.0)
    ctx = nrm(ks[2], (BATCH, CTX_LEN, D_MODEL), 1.0)
    c_ctx = nrm(ks[3], (D_MODEL,), 1.0)
    norm_w = 1.0 + nrm(ks[4], (DEPTH, D_MODEL), 0.02)
    w_mod = nrm(ks[5], (DEPTH, D_MODEL, 3 * D_MODEL), 0.5 * D_MODEL ** -0.5)
    b_mod = nrm(ks[6], (DEPTH, 3 * D_MODEL), 0.01)
    w_in = nrm(ks[7], (DEPTH, D_MODEL, N_IN), D_MODEL ** -0.5)
    conv_w = nrm(ks[8], (DEPTH, SHORT_CONV, GDN_QKV), SHORT_CONV ** -0.5)
    a_log = jnp.log(jax.random.uniform(ks[9], (DEPTH, 2, GDN_HEADS), f32, 1.0, 16.0))
    dt = jnp.exp(jax.random.uniform(ks[10], (DEPTH, 2, GDN_HEADS), f32, math.log(1e-3), math.log(1e-1)))
    dt_bias = dt + jnp.log(-jnp.expm1(-dt))
    gdn_norm_w = 1.0 + nrm(ks[11], (DEPTH, GDN_DV), 0.02)
    q_norm_w = 1.0 + nrm(ks[12], (DEPTH, ATT_HD), 0.02)
    k_norm_w = 1.0 + nrm(ks[13], (DEPTH, ATT_HD), 0.02)
    sink = nrm(ks[14], (DEPTH, ATT_HEADS), 0.5)
    w_proj_a = nrm(ks[15], (DEPTH, GDN_WIDTH, D_MODEL), GDN_WIDTH ** -0.5)
    w_proj_b = nrm(ks[16], (DEPTH, ATT_WIDTH, D_MODEL), ATT_WIDTH ** -0.5)
    w_out = nrm(ks[17], (DEPTH, D_MODEL, D_MODEL), D_MODEL ** -0.5)
    return {'x': x, 'c': c, 'ctx': ctx, 'c_ctx': c_ctx, 'norm_w': norm_w, 'w_mod': w_mod,
            'b_mod': b_mod, 'w_in': w_in, 'conv_w': conv_w, 'a_log': a_log, 'dt_bias': dt_bias,
            'gdn_norm_w': gdn_norm_w, 'q_norm_w': q_norm_w, 'k_norm_w': k_norm_w, 'sink': sink,
            'w_proj_a': w_proj_a, 'w_proj_b': w_proj_b, 'w_out': w_out}


def reference(x, c, ctx, c_ctx, norm_w, w_mod, b_mod, w_in, conv_w, a_log, dt_bias, gdn_norm_w,
              q_norm_w, k_norm_w, sink, w_proj_a, w_proj_b, w_out):
    l = x.shape[1]
    rows = l // GRID_W
    pos_row = jnp.repeat(jnp.arange(rows, dtype=jnp.float32), GRID_W)
    pos_col = jnp.tile(jnp.arange(GRID_W, dtype=jnp.float32), rows)
    for i in range(DEPTH):
        x, ctx = _layer(x, ctx, c, c_ctx, pos_row, pos_col, norm_w[i], w_mod[i], b_mod[i], w_in[i],
                        conv_w[i], a_log[i], dt_bias[i], gdn_norm_w[i], q_norm_w[i], k_norm_w[i],
                        sink[i], w_proj_a[i], w_proj_b[i], w_out[i], update_ctx=(i < DEPTH - 1))
    return x
```

```python
import functools
import math

import numpy as np
import jax
import jax.numpy as jnp
from jax import lax
from jax.experimental import pallas as pl
from jax.experimental.pallas import tpu as pltpu

F32 = jnp.float32
BF16 = jnp.bfloat16

D_MODEL = 1024
DEPTH = 2
GRID_W = 64
EPS = 1e-6
NEG_INF = -1e30

GDN_HEADS = 4
GDN_DK = 128
GDN_DV = 128
GDN_QKV = GDN_HEADS * (2 * GDN_DK + GDN_DV)
GDN_WIDTH = GDN_HEADS * GDN_DV
SHORT_CONV = 5
CHUNK = 64

ATT_HEADS = 8
ATT_KV_HEADS = 2
ATT_GROUP = ATT_HEADS // ATT_KV_HEADS
ATT_HD = 64
ATT_WIDTH = ATT_HEADS * ATT_HD
ATT_KV_WIDTH = ATT_KV_HEADS * ATT_HD
ATT_BLOCK = 128
ROPE_BASE = 10000.0
AXIS_DIM = ATT_HD // 2
ROPE_F = AXIS_DIM // 2

N_STATE = GDN_QKV + 4 * GDN_HEADS + 2 * ATT_KV_WIDTH

LANES = 128
P_QKV = 0
P_GATE = GDN_QKV
P_K = P_GATE + LANES
P_V = P_K + ATT_KV_WIDTH
P_STATE_END = P_V + ATT_KV_WIDTH
P_ZA = P_STATE_END
P_Q = P_ZA + GDN_WIDTH
P_ZB = P_Q + ATT_WIDTH
P_GA = P_ZB + ATT_WIDTH
P_GB = P_GA + D_MODEL
P_END = P_GB + D_MODEL

VMEM_LIMIT = 56 * 1024 * 1024


def _cparams(sem):
    return pltpu.CompilerParams(dimension_semantics=sem, vmem_limit_bytes=VMEM_LIMIT)


def _k_perm():
    idx = np.zeros(ATT_KV_WIDTH, np.int32)
    for slab in range(4):
        for kvh in range(ATT_KV_HEADS):
            for f in range(ROPE_F):
                idx[slab * 32 + kvh * 16 + f] = kvh * ATT_HD + slab * ROPE_F + f
    return idx


def _q_perm():
    idx = np.zeros(ATT_WIDTH, np.int32)
    for g in range(ATT_GROUP):
        for slab in range(4):
            for kvh in range(ATT_KV_HEADS):
                for f in range(ROPE_F):
                    idx[g * 128 + slab * 32 + kvh * 16 + f] = (kvh * ATT_GROUP + g) * ATT_HD + slab * ROPE_F + f
    return idx


def _o_perm():
    idx = np.zeros(ATT_WIDTH, np.int32)
    for g in range(ATT_GROUP):
        for kvh in range(ATT_KV_HEADS):
            for d in range(ATT_HD):
                idx[g * 128 + kvh * ATT_HD + d] = (kvh * ATT_GROUP + g) * ATT_HD + d
    return idx


def _norm_perm():
    idx = np.zeros(LANES, np.int32)
    for slab in range(4):
        for kvh in range(ATT_KV_HEADS):
            for f in range(ROPE_F):
                idx[slab * 32 + kvh * 16 + f] = slab * ROPE_F + f
    return idx


def _pack_w_in(w_in):
    o = 0
    qkv = w_in[:, o:o + GDN_QKV]; o += GDN_QKV
    gates = w_in[:, o:o + 4 * GDN_HEADS]; o += 4 * GDN_HEADS
    kb = w_in[:, o:o + ATT_KV_WIDTH]; o += ATT_KV_WIDTH
    vb = w_in[:, o:o + ATT_KV_WIDTH]; o += ATT_KV_WIDTH
    za = w_in[:, o:o + GDN_WIDTH]; o += GDN_WIDTH
    qb = w_in[:, o:o + ATT_WIDTH]; o += ATT_WIDTH
    zb = w_in[:, o:o + ATT_WIDTH]; o += ATT_WIDTH
    ga = w_in[:, o:o + D_MODEL]; o += D_MODEL
    gb = w_in[:, o:o + D_MODEL]
    gates = jnp.pad(gates, ((0, 0), (0, LANES - 4 * GDN_HEADS)))
    packed = jnp.concatenate(
        [qkv, gates, kb[:, _k_perm()], vb, za, qb[:, _q_perm()], zb[:, _o_perm()], ga, gb], axis=1)
    return packed.astype(BF16)


def _rope_tables(seq_len):
    t = jnp.arange(seq_len, dtype=F32)
    pos_row = jnp.floor(t / GRID_W)
    pos_col = t - pos_row * GRID_W
    inv = ROPE_BASE ** (-jnp.arange(0, AXIS_DIM, 2, dtype=F32) / AXIS_DIM)
    ang_r = pos_row[:, None] * inv[None, :]
    ang_c = pos_col[:, None] * inv[None, :]

    def lay(a):
        return jnp.concatenate([a, a], axis=1)

    cos = jnp.concatenate([lay(jnp.cos(ang_r)), lay(jnp.cos(ang_r)), lay(jnp.cos(ang_c)), lay(jnp.cos(ang_c))], axis=1)
    sin = jnp.concatenate([-lay(jnp.sin(ang_r)), lay(jnp.sin(ang_r)), -lay(jnp.sin(ang_c)), lay(jnp.sin(ang_c))], axis=1)
    return cos, sin


def _mod_kernel(c_ref, w_ref, b_ref, o_ref):
    c = c_ref[...]
    s = c * jax.nn.sigmoid(c)
    o_ref[0] = jnp.dot(s.astype(BF16), w_ref[0].astype(BF16), preferred_element_type=F32) + b_ref[0]


def _mod_call(c_all, w_mod, b_mod):
    rows = c_all.shape[0]
    nblk = 3
    return pl.pallas_call(
        _mod_kernel,
        grid=(DEPTH, nblk),
        in_specs=[
            pl.BlockSpec((rows, D_MODEL), lambda l, j: (0, 0)),
            pl.BlockSpec((1, D_MODEL, D_MODEL), lambda l, j: (l, 0, j)),
            pl.BlockSpec((1, 1, D_MODEL), lambda l, j: (l, 0, j)),
        ],
        out_specs=pl.BlockSpec((1, rows, D_MODEL), lambda l, j: (l, 0, j)),
        out_shape=jax.ShapeDtypeStruct((DEPTH, rows, 3 * D_MODEL), F32),
        compiler_params=_cparams(("arbitrary", "arbitrary")),
        name="mod",
    )(c_all, w_mod, b_mod.reshape(DEPTH, 1, 3 * D_MODEL))


def _swap32(t):
    lane = lax.broadcasted_iota(jnp.int32, t.shape, 1)
    even = (lane // 32) % 2 == 0
    return jnp.where(even, pltpu.roll(t, 96, 1), pltpu.roll(t, 32, 1))


def _headnorm_rope(t, nw, gmat, cos, sin):
    sq = t * t
    hi = sq.astype(BF16)
    lo = (sq - hi.astype(F32)).astype(BF16)
    ss = jnp.dot(hi, gmat, preferred_element_type=F32) + jnp.dot(lo, gmat, preferred_element_type=F32)
    tn = t * lax.rsqrt(ss * (1.0 / ATT_HD) + EPS) * nw
    return tn * cos + _swap32(tn) * sin


def _softplus(x):
    return jnp.maximum(x, 0.0) + jnp.log1p(jnp.exp(-jnp.abs(x)))


def _proj_kernel(x_ref, scale_ref, shift_ref, normw_ref, w_ref, gvec_ref, qn_ref, kn_ref, gmat_ref,
                 cos_ref, sin_ref, *out_refs, with_rest):
    if with_rest:
        qkv_ref, gates_ref, kv_ref, za_ref, q_ref, zb_ref, ga_ref, gb_ref = out_refs
    else:
        qkv_ref, gates_ref, kv_ref = out_refs
    x = x_ref[0]
    ms = jnp.mean(x * x, axis=-1, keepdims=True)
    a = normw_ref[...] * (1.0 + scale_ref[0])
    h = (x * lax.rsqrt(ms + EPS) * a + shift_ref[0]).astype(BF16)

    def proj(lo, width):
        return jnp.dot(h, w_ref[:, lo:lo + width], preferred_element_type=F32)

    for c in range(GDN_QKV // 512):
        qkv_ref[0, :, c * 512:(c + 1) * 512] = proj(P_QKV + c * 512, 512).astype(BF16)

    g = proj(P_GATE, LANES)
    lane = lax.broadcasted_iota(jnp.int32, g.shape, 1)
    beta = jax.nn.sigmoid(g)
    dec = -gvec_ref[0:1, :] * _softplus(g + gvec_ref[1:2, :])
    gates_ref[0] = jnp.where(lane < 2 * GDN_HEADS, beta, dec)

    gmat = gmat_ref[...]
    cos = cos_ref[...]
    sin = sin_ref[...]
    k = _headnorm_rope(proj(P_K, LANES), kn_ref[...], gmat, cos, sin)
    kv_ref[0, :, 0:LANES] = k.astype(BF16)
    kv_ref[0, :, LANES:2 * LANES] = proj(P_V, LANES).astype(BF16)

    if with_rest:
        z = proj(P_ZA, GDN_WIDTH)
        za_ref[0] = (z * jax.nn.sigmoid(z)).astype(BF16)
        for gi in range(ATT_GROUP):
            qg = _headnorm_rope(proj(P_Q + gi * LANES, LANES), qn_ref[...], gmat, cos, sin)
            q_ref[0, :, gi * LANES:(gi + 1) * LANES] = qg.astype(BF16)
        z = proj(P_ZB, ATT_WIDTH)
        zb_ref[0] = (z * jax.nn.sigmoid(z)).astype(BF16)
        for c in range(D_MODEL // 512):
            ga_ref[0, :, c * 512:(c + 1) * 512] = jax.nn.sigmoid(proj(P_GA + c * 512, 512)).astype(BF16)
            gb_ref[0, :, c * 512:(c + 1) * 512] = jax.nn.sigmoid(proj(P_GB + c * 512, 512)).astype(BF16)


def _proj_call(x, scale, shift, normw, w_packed, gvec, qn, kn, gmat, cos, sin, *, with_rest, tm):
    b, l, _ = x.shape
    nw = P_END if with_rest else P_STATE_END
    per_batch_mod = scale.shape[0] == b and b > 1
    mod_map = (lambda i, j: (i, 0, 0)) if per_batch_mod else (lambda i, j: (0, 0, 0))
    const2 = lambda i, j: (0, 0)
    row_map = lambda i, j: (i, j, 0)
    in_specs = [
        pl.BlockSpec((1, tm, D_MODEL), row_map),
        pl.BlockSpec((1, 1, D_MODEL), mod_map),
        pl.BlockSpec((1, 1, D_MODEL), mod_map),
        pl.BlockSpec((1, D_MODEL), const2),
        pl.BlockSpec((D_MODEL, nw), const2),
        pl.BlockSpec((8, LANES), const2),
        pl.BlockSpec((1, LANES), const2),
        pl.BlockSpec((1, LANES), const2),
        pl.BlockSpec((LANES, LANES), const2),
        pl.BlockSpec((tm, LANES), lambda i, j: (j, 0)),
        pl.BlockSpec((tm, LANES), lambda i, j: (j, 0)),
    ]
    widths = [(GDN_QKV, BF16), (LANES, F32), (2 * LANES, BF16)]
    if with_rest:
        widths += [(GDN_WIDTH, BF16), (ATT_WIDTH, BF16), (ATT_WIDTH, BF16), (D_MODEL, BF16), (D_MODEL, BF16)]
    out_specs = [pl.BlockSpec((1, tm, w), row_map) for w, _ in widths]
    out_shape = [jax.ShapeDtypeStruct((b, l, w), dt) for w, dt in widths]
    return pl.pallas_call(
        functools.partial(_proj_kernel, with_rest=with_rest),
        grid=(b, l // tm),
        in_specs=in_specs,
        out_specs=out_specs,
        out_shape=out_shape,
        compiler_params=_cparams(("arbitrary", "arbitrary")),
        name="proj_full" if with_rest else "proj_state",
    )(x, scale, shift, normw, w_packed[:, :nw], gvec, qn, kn, gmat, cos, sin)


def _conv_kernel(x_ref, w_ref, o_ref):
    j = pl.program_id(1)
    l = x_ref.shape[1]
    x = x_ref[0].astype(F32)
    pad = jnp.zeros((8, LANES), F32)
    xp = jnp.concatenate([pad, x, pad], axis=0)
    w = w_ref[...]
    half = SHORT_CONV // 2
    y = jnp.zeros((l, LANES), F32)
    for tap in range(SHORT_CONV):
        y = y + xp[8 - half + tap:8 - half + tap + l, :] * w[tap:tap + 1, :]
    y = y * jax.nn.sigmoid(y)
    ss = jnp.sum(y * y, axis=-1, keepdims=True)
    yn = y * lax.rsqrt(ss + EPS)
    kind = j // GDN_HEADS
    qscale = jnp.where(kind == 0, GDN_DK ** -0.5, 1.0).astype(F32)
    o_ref[0] = jnp.where(kind < 2, yn * qscale, y).astype(BF16)


def _conv_call(qkv_raw, conv_w):
    b, l, _ = qkv_raw.shape
    nblk = GDN_QKV // LANES
    return pl.pallas_call(
        _conv_kernel,
        grid=(b, nblk),
        in_specs=[
            pl.BlockSpec((1, l, LANES), lambda i, j: (i, 0, j)),
            pl.BlockSpec((8, LANES), lambda i, j: (0, j)),
        ],
        out_specs=pl.BlockSpec((1, l, LANES), lambda i, j: (i, 0, j)),
        out_shape=jax.ShapeDtypeStruct((b, l, GDN_QKV), BF16),
        compiler_params=_cparams(("arbitrary", "arbitrary")),
        name="conv",
    )(qkv_raw, conv_w)


def _blockdiag(y, isf):
    zero = jnp.zeros_like(y)
    return jnp.concatenate([jnp.where(isf, y, zero), jnp.where(isf, zero, y)], axis=0)


def _prep_kernel(qkv_ref, gates_ref, u_ref, w_ref, kdt_ref, qd_ref, qk_ref, eg_ref, *, tc):
    H = GDN_HEADS
    C = CHUNK
    G = gates_ref[0]
    gT = G.T[2 * H:4 * H, :]
    lane_t = lax.broadcasted_iota(jnp.int32, gT.shape, 1)
    pos = lane_t % C
    pre = gT
    suf = gT
    for s in (1, 2, 4, 8, 16, 32):
        pre = pre + jnp.where(pos >= s, pltpu.roll(pre, s, 1), 0.0)
        suf = suf + jnp.where(pos < C - s, pltpu.roll(suf, tc - s, 1), 0.0)
    row8 = lax.broadcasted_iota(jnp.int32, gT.shape, 0)
    gcT = jnp.where(row8 < H, pre, suf)
    gc = jnp.concatenate([gcT, jnp.zeros((LANES - 2 * H, tc), F32)], axis=0).T

    lane = lax.broadcasted_iota(jnp.int32, (C, LANES), 1)
    ii = lax.broadcasted_iota(jnp.int32, (C, LANES), 0)
    jj = lane % C
    isf = lane < C
    isb = jnp.logical_not(isf)
    incl = jnp.logical_or(jnp.logical_and(isf, ii >= jj), jnp.logical_and(isb, ii <= jj))
    strict = jnp.logical_or(jnp.logical_and(isf, ii > jj), jnp.logical_and(isb, ii < jj))
    eye2 = (ii == jj).astype(F32)
    same = {kk: (ii // kk) == (jj // kk) for kk in (2, 4, 8, 16, 32, 64)}
    lane8 = lax.broadcasted_iota(jnp.int32, (1, LANES), 1)

    def colb(arr, r0, c):
        return jnp.broadcast_to(arr[r0:r0 + C, c:c + 1], (C, LANES))

    for s in range(tc // C):
        r0 = s * C
        glrow = jnp.where(lane8 < H, gc[r0 + C - 1:r0 + C, :], gc[r0:r0 + 1, :])
        eg_ref[0, r0:r0 + C, :] = jnp.broadcast_to(jnp.exp(glrow), (C, LANES))

    for lt in range(tc // LANES):
        gt = gcT[:, lt * LANES:(lt + 1) * LANES]
        gt_r = pltpu.roll(gt, C, 1)
        for h in range(H):
            kd_pair = []
            for halfsel in range(2):
                r0 = (2 * lt + halfsel) * C
                if halfsel == 0:
                    row_f, row_b = gt[h:h + 1, :], gt_r[H + h:H + h + 1, :]
                else:
                    row_f, row_b = gt_r[h:h + 1, :], gt[H + h:H + h + 1, :]
                gc_row2 = jnp.broadcast_to(jnp.where(lane8 < C, row_f, row_b), (C, LANES))
                gcf = colb(gc, r0, h)
                gcb = colb(gc, r0, H + h)
                gc_col2 = jnp.where(isf, gcf, gcb)
                dec = jnp.where(incl, jnp.exp(jnp.where(incl, gc_col2 - gc_row2, 0.0)), 0.0)
                bf = colb(G, r0, h)
                bb = colb(G, r0, H + h)
                beta2 = jnp.where(isf, bf, bb)

                qt = qkv_ref[0, r0:r0 + C, h * LANES:(h + 1) * LANES]
                kt = qkv_ref[0, r0:r0 + C, (H + h) * LANES:(H + h + 1) * LANES]
                vt = qkv_ref[0, r0:r0 + C, (2 * H + h) * LANES:(2 * H + h + 1) * LANES]
                lhs = jnp.concatenate([kt, qt], axis=0)
                kk2 = jnp.concatenate([kt, kt], axis=0)
                kq = lax.dot_general(lhs, kk2, (((1,), (1,)), ((), ())), preferred_element_type=F32)
                a2 = jnp.where(strict, beta2 * kq[0:C] * dec, 0.0)
                qk2 = kq[C:2 * C] * dec

                t2 = eye2 - jnp.where(same[2], a2, 0.0)
                for kk in (2, 4, 8, 16, 32):
                    e = jnp.where(jnp.logical_and(same[2 * kk], jnp.logical_not(same[kk])), a2, 0.0)
                    tb = t2.astype(BF16)
                    p = jnp.dot(e.astype(BF16), _blockdiag(tb, isf), preferred_element_type=F32)
                    t2 = t2 - jnp.dot(tb, _blockdiag(p.astype(BF16), isf), preferred_element_type=F32)

                kf = kt.astype(F32)
                vf = vt.astype(F32)
                qf = qt.astype(F32)
                egf = jnp.exp(gcf)
                egb = jnp.exp(gcb)
                z2 = jnp.zeros((C, 2 * LANES), F32)
                rf = jnp.concatenate([vf * bf, kf * (bf * egf), z2], axis=1)
                rb = jnp.concatenate([z2, vf * bb, kf * (bb * egb)], axis=1)
                rhs = jnp.concatenate([rf, rb], axis=0).astype(BF16)
                uw = jnp.dot(t2.astype(BF16), rhs, preferred_element_type=F32)

                glf = jnp.broadcast_to(gc[r0 + C - 1:r0 + C, h:h + 1], (C, LANES))
                glb = jnp.broadcast_to(gc[r0:r0 + 1, H + h:H + h + 1], (C, LANES))
                kd_pair.append((kf * jnp.exp(glf - gcf), kf * jnp.exp(glb - gcb)))

                cf = h * LANES
                cb = (H + h) * LANES
                u_ref[0, r0:r0 + C, cf:cf + LANES] = uw[:, 0:LANES].astype(BF16)
                w_ref[0, r0:r0 + C, cf:cf + LANES] = uw[:, LANES:2 * LANES].astype(BF16)
                u_ref[0, r0:r0 + C, cb:cb + LANES] = uw[:, 2 * LANES:3 * LANES].astype(BF16)
                w_ref[0, r0:r0 + C, cb:cb + LANES] = uw[:, 3 * LANES:4 * LANES].astype(BF16)
                qd_ref[0, r0:r0 + C, cf:cf + LANES] = (qf * egf).astype(BF16)
                qd_ref[0, r0:r0 + C, cb:cb + LANES] = (qf * egb).astype(BF16)
                qk_ref[0, r0:r0 + C, h * LANES:(h + 1) * LANES] = qk2.astype(BF16)
            for d in range(2):
                kdt = jnp.concatenate([kd_pair[0][d], kd_pair[1][d]], axis=0).T
                kdt_ref[0, (d * H + h) * LANES:(d * H + h + 1) * LANES, lt * LANES:(lt + 1) * LANES] = kdt.astype(BF16)


def _prep_call(qkv, gates, *, tc):
    b, l, _ = qkv.shape
    row_map = lambda i, j: (i, j, 0)
    wide = 2 * GDN_HEADS * LANES
    return pl.pallas_call(
        functools.partial(_prep_kernel, tc=tc),
        grid=(b, l // tc),
        in_specs=[pl.BlockSpec((1, tc, GDN_QKV), row_map), pl.BlockSpec((1, tc, LANES), row_map)],
        out_specs=[
            pl.BlockSpec((1, tc, wide), row_map),
            pl.BlockSpec((1, tc, wide), row_map),
            pl.BlockSpec((1, wide, tc), lambda i, j: (i, 0, j)),
            pl.BlockSpec((1, tc, wide), row_map),
            pl.BlockSpec((1, tc, GDN_HEADS * LANES), row_map),
            pl.BlockSpec((1, tc, LANES), row_map),
        ],
        out_shape=[
            jax.ShapeDtypeStruct((b, l, wide), BF16),
            jax.ShapeDtypeStruct((b, l, wide), BF16),
            jax.ShapeDtypeStruct((b, wide, l), BF16),
            jax.ShapeDtypeStruct((b, l, wide), BF16),
            jax.ShapeDtypeStruct((b, l, GDN_HEADS * LANES), BF16),
            jax.ShapeDtypeStruct((b, l, LANES), F32),
        ],
        compiler_params=_cparams(("arbitrary", "arbitrary")),
        name="gdn_prep",
    )(qkv, gates)


def _chain_kernel(uf_ref, wf_ref, kdtf_ref, qdf_ref, qkf_ref, egf_ref,
                  ub_ref, wb_ref, kdtb_ref, qdb_ref, qkb_ref, egb_ref, s0_ref,
                  of_ref, ob_ref, sfin_ref, s_scr, *, tc):
    H = GDN_HEADS
    C = CHUNK
    i = pl.program_id(1)
    n = pl.num_programs(1)

    @pl.when(i == 0)
    def _():
        s_scr[...] = s0_ref[0]

    nch = tc // C
    zc = jnp.zeros((C, LANES), BF16)
    for c in range(nch):
        for d in range(2):
            cc = c if d == 0 else nch - 1 - c
            r0 = cc * C
            lt, par = divmod(cc, 2)
            u_ref, w_ref, kdt_ref, qd_ref, qk_ref, eg_ref, o_ref = (
                (uf_ref, wf_ref, kdtf_ref, qdf_ref, qkf_ref, egf_ref, of_ref) if d == 0 else
                (ub_ref, wb_ref, kdtb_ref, qdb_ref, qkb_ref, egb_ref, ob_ref))
            for h in range(H):
                cs = h * LANES
                s_old = s_scr[d, h]
                lhs = jnp.concatenate([w_ref[0, r0:r0 + C, cs:cs + LANES], qd_ref[0, r0:r0 + C, cs:cs + LANES]], axis=0)
                r = jnp.dot(lhs, s_old.astype(BF16), preferred_element_type=F32)
                vnew = u_ref[0, r0:r0 + C, cs:cs + LANES].astype(F32) - r[0:C]
                vb = vnew.astype(BF16)
                v_dir = jnp.concatenate([vb, zc], axis=0) if d == 0 else jnp.concatenate([zc, vb], axis=0)
                o = r[C:2 * C] + jnp.dot(qk_ref[0, r0:r0 + C, cs:cs + LANES], v_dir, preferred_element_type=F32)
                o_ref[0, r0:r0 + C, cs:cs + LANES] = o
                v_par = jnp.concatenate([vb, zc], axis=0) if par == 0 else jnp.concatenate([zc, vb], axis=0)
                kdt = kdt_ref[0, cs:cs + LANES, lt * LANES:(lt + 1) * LANES]
                eg = jnp.broadcast_to(eg_ref[0, r0:r0 + 1, d * H + h:d * H + h + 1], (LANES, LANES))
                s_scr[d, h] = s_old * eg + jnp.dot(kdt, v_par, preferred_element_type=F32)

    @pl.when(i == n - 1)
    def _():
        sfin_ref[0] = s_scr[...]


def _chain_call(u, w, kdt, qd, qk, eg, s0, *, tc):
    b, l, _ = u.shape
    n = l // tc
    hw = GDN_HEADS * LANES
    fwd = lambda i, j: (i, j, 0)
    bwd = lambda i, j: (i, n - 1 - j, 1)
    bwd0 = lambda i, j: (i, n - 1 - j, 0)
    in_specs = [
        pl.BlockSpec((1, tc, hw), fwd), pl.BlockSpec((1, tc, hw), fwd),
        pl.BlockSpec((1, hw, tc), lambda i, j: (i, 0, j)),
        pl.BlockSpec((1, tc, hw), fwd), pl.BlockSpec((1, tc, hw), fwd), pl.BlockSpec((1, tc, LANES), fwd),
        pl.BlockSpec((1, tc, hw), bwd), pl.BlockSpec((1, tc, hw), bwd),
        pl.BlockSpec((1, hw, tc), lambda i, j: (i, 1, n - 1 - j)),
        pl.BlockSpec((1, tc, hw), bwd), pl.BlockSpec((1, tc, hw), bwd0), pl.BlockSpec((1, tc, LANES), bwd0),
        pl.BlockSpec((1, 2, GDN_HEADS, LANES, LANES), lambda i, j: (i, 0, 0, 0, 0)),
    ]
    out_specs = [
        pl.BlockSpec((1, tc, hw), fwd),
        pl.BlockSpec((1, tc, hw), bwd0),
        pl.BlockSpec((1, 2, GDN_HEADS, LANES, LANES), lambda i, j: (i, 0, 0, 0, 0)),
    ]
    out_shape = [
        jax.ShapeDtypeStruct((b, l, hw), F32),
        jax.ShapeDtypeStruct((b, l, hw), F32),
        jax.ShapeDtypeStruct((b, 2, GDN_HEADS, LANES, LANES), F32),
    ]
    return pl.pallas_call(
        functools.partial(_chain_kernel, tc=tc),
        grid=(b, n),
        in_specs=in_specs,
        out_specs=out_specs,
        out_shape=out_shape,
        scratch_shapes=[pltpu.VMEM((2, GDN_HEADS, LANES, LANES), F32)],
        compiler_params=_cparams(("arbitrary", "arbitrary")),
        name="gdn_chain",
    )(u, w, kdt, qd, qk, eg, u, w, kdt, qd, qk, eg, s0)


def _attn_kernel(sink_ref, q_ref, kvp_ref, kvo_ref, kvn_ref, kvc_ref, zb_ref, o_ref, *, local):
    nb = pl.num_programs(1)
    n = pl.program_id(1)
    T = ATT_BLOCK
    q = q_ref[0]
    qs = jnp.concatenate([q[:, g * LANES:(g + 1) * LANES] for g in range(ATT_GROUP)], axis=0)
    lane = lax.broadcasted_iota(jnp.int32, (1, LANES), 1)
    rowi = lax.broadcasted_iota(jnp.int32, (ATT_GROUP * T, T), 0) % T
    colj = lax.broadcasted_iota(jnp.int32, (ATT_GROUP * T, T), 1)
    grp = lax.broadcasted_iota(jnp.int32, (ATT_GROUP * T, 1), 0) // T

    pieces = [(kvc_ref, None)]
    if local:
        pieces += [(kvp_ref, jnp.logical_and(colj >= rowi, n > 0)),
                   (kvo_ref, None),
                   (kvn_ref, jnp.logical_and(colj <= rowi, n < nb - 1))]

    acc = jnp.zeros((ATT_GROUP * T, LANES), F32)
    for kvh in range(ATT_KV_HEADS):
        kmask = ((lane // ROPE_F) % ATT_KV_HEADS == kvh)
        vmask = (lane // ATT_HD == kvh)
        sink = jnp.zeros((ATT_GROUP * T, 1), F32)
        for g in range(ATT_GROUP):
            sink = jnp.where(grp == g, sink_ref[kvh * ATT_GROUP + g], sink)
        scores = []
        m = sink
        for ref, valid in pieces:
            kv = ref[0]
            kh = jnp.where(kmask, kv[:, 0:LANES], jnp.zeros_like(kv[:, 0:LANES]))
            s = lax.dot_general(qs, kh, (((1,), (1,)), ((), ())), preferred_element_type=F32)
            if valid is not None:
                s = jnp.where(valid, s, NEG_INF)
            scores.append(s)
            m = jnp.maximum(m, jnp.max(s, axis=-1, keepdims=True))
        denom = jnp.exp(sink - m)
        o_h = jnp.zeros((ATT_GROUP * T, LANES), F32)
        for (ref, _), s in zip(pieces, scores):
            kv = ref[0]
            vh = jnp.where(vmask, kv[:, LANES:2 * LANES], jnp.zeros_like(kv[:, 0:LANES]))
            p = jnp.exp(s - m)
            denom = denom + jnp.sum(p, axis=-1, keepdims=True)
            o_h = o_h + jnp.dot(p.astype(BF16), vh, preferred_element_type=F32)
        acc = acc + o_h / denom
    for g in range(ATT_GROUP):
        og = acc[g * T:(g + 1) * T, :] * zb_ref[0, :, g * LANES:(g + 1) * LANES].astype(F32)
        o_ref[0, :, g * LANES:(g + 1) * LANES] = og.astype(BF16)


def _attn_call(sink, q, kv, kvc, zb, *, local):
    b, l, _ = q.shape
    nb = l // ATT_BLOCK
    lc = kvc.shape[1]
    blk = lambda f: pl.BlockSpec((1, ATT_BLOCK, 2 * LANES), f)
    return pl.pallas_call(
        functools.partial(_attn_kernel, local=local),
        grid=(b, nb),
        in_specs=[
            pl.BlockSpec(memory_space=pltpu.SMEM),
            pl.BlockSpec((1, ATT_BLOCK, ATT_WIDTH), lambda i, j: (i, j, 0)),
            blk(lambda i, j: (i, jnp.maximum(j - 1, 0), 0)),
            blk(lambda i, j: (i, j, 0)),
            blk(lambda i, j: (i, jnp.minimum(j + 1, nb - 1), 0)),
            pl.BlockSpec((1, lc, 2 * LANES), lambda i, j: (i, 0, 0)),
            pl.BlockSpec((1, ATT_BLOCK, ATT_WIDTH), lambda i, j: (i, j, 0)),
        ],
        out_specs=pl.BlockSpec((1, ATT_BLOCK, ATT_WIDTH), lambda i, j: (i, j, 0)),
        out_shape=jax.ShapeDtypeStruct((b, l, ATT_WIDTH), BF16),
        compiler_params=_cparams(("arbitrary", "arbitrary")),
        name="attn_local" if local else "attn_ctx",
    )(sink, q, kv, kv, kv, kvc, zb)


def _merge_kernel(x_ref, gate_ref, of_ref, ob_ref, za_ref, yb_ref, ga_ref, gb_ref, gnw_ref,
                  wpa_ref, wpb_ref, wout_ref, o_ref):
    gnw = gnw_ref[...]
    ya_parts = []
    for h in range(GDN_HEADS):
        sl = slice(h * LANES, (h + 1) * LANES)
        o = of_ref[0, :, sl] + ob_ref[0, :, sl]
        on = o * lax.rsqrt(jnp.mean(o * o, axis=-1, keepdims=True) + EPS) * gnw
        ya_parts.append((on * za_ref[0, :, sl].astype(F32)).astype(BF16))
    ya = jnp.concatenate(ya_parts, axis=1)
    pa = jnp.dot(ya, wpa_ref[...], preferred_element_type=F32)
    pb = jnp.dot(yb_ref[0], wpb_ref[...], preferred_element_type=F32)
    y = ga_ref[0].astype(F32) * pa + gb_ref[0].astype(F32) * pb
    out = jnp.dot(y.astype(BF16), wout_ref[...], preferred_element_type=F32)
    o_ref[0] = x_ref[0] + gate_ref[0] * out


def _merge_call(x, gate, of, ob, za, yb, ga, gb, gnw, wpa, wpb, wout, *, tm):
    b, l, _ = x.shape
    per_batch_mod = gate.shape[0] == b and b > 1
    mod_map = (lambda i, j: (i, 0, 0)) if per_batch_mod else (lambda i, j: (0, 0, 0))
    row = lambda w: pl.BlockSpec((1, tm, w), lambda i, j: (i, j, 0))
    const2 = lambda i, j: (0, 0)
    return pl.pallas_call(
        _merge_kernel,
        grid=(b, l // tm),
        in_specs=[
            row(D_MODEL), pl.BlockSpec((1, 1, D_MODEL), mod_map),
            row(GDN_WIDTH), row(GDN_WIDTH), row(GDN_WIDTH), row(ATT_WIDTH), row(D_MODEL), row(D_MODEL),
            pl.BlockSpec((1, LANES), const2),
            pl.BlockSpec((GDN_WIDTH, D_MODEL), const2),
            pl.BlockSpec((ATT_WIDTH, D_MODEL), const2),
            pl.BlockSpec((D_MODEL, D_MODEL), const2),
        ],
        out_specs=row(D_MODEL),
        out_shape=jax.ShapeDtypeStruct((b, l, D_MODEL), F32),
        compiler_params=_cparams(("arbitrary", "arbitrary")),
        name="merge",
    )(x, gate, of, ob, za, yb, ga, gb, gnw, wpa, wpb, wout)


def _pick_tile(l, pref):
    t = min(pref, l)
    while l % t:
        t //= 2
    return t


def kernel(x, c, ctx, c_ctx, norm_w, w_mod, b_mod, w_in, conv_w, a_log, dt_bias, gdn_norm_w,
           q_norm_w, k_norm_w, sink, w_proj_a, w_proj_b, w_out):
    b, l, _ = x.shape
    lc = ctx.shape[1]
    assert l % ATT_BLOCK == 0 and lc % ATT_BLOCK == 0 and x.shape[2] == D_MODEL

    rows = ((b + 1 + 7) // 8) * 8
    c_all = jnp.concatenate([c, c_ctx[None, :], jnp.zeros((rows - b - 1, D_MODEL), F32)], axis=0)
    mod = _mod_call(c_all, w_mod, b_mod)

    cos_l, sin_l = _rope_tables(l)
    cos_c = jnp.ones((lc, LANES), F32)
    sin_c = jnp.zeros((lc, LANES), F32)
    grp = (np.arange(LANES) // ROPE_F) % ATT_KV_HEADS
    gmat = jnp.asarray((grp[:, None] == grp[None, :]).astype(np.float32), dtype=BF16)
    nperm = _norm_perm()
    operm = _o_perm()
    s0 = jnp.zeros((b, 2, GDN_HEADS, LANES, LANES), F32)
    tm_l = _pick_tile(l, 256)
    tm_c = _pick_tile(lc, 256)
    tc_l = _pick_tile(l, 256)
    tc_c = _pick_tile(lc, 256)

    for i in range(DEPTH):
        update_ctx = i < DEPTH - 1
        shift, scale, gate = (mod[i, :, k * D_MODEL:(k + 1) * D_MODEL] for k in range(3))
        shift_l, scale_l, gate_l = (t[:b, None, :] for t in (shift, scale, gate))
        shift_c, scale_c, gate_c = (t[b:b + 1, None, :] for t in (shift, scale, gate))
        w_packed = _pack_w_in(w_in[i])
        normw = norm_w[i][None, :]
        gvec = jnp.zeros((8, LANES), F32)
        gvec = gvec.at[0, 2 * GDN_HEADS:4 * GDN_HEADS].set(jnp.exp(a_log[i].astype(F32)).reshape(-1))
        gvec = gvec.at[1, 2 * GDN_HEADS:4 * GDN_HEADS].set(dt_bias[i].astype(F32).reshape(-1))
        qn = (q_norm_w[i][nperm] * (ATT_HD ** -0.5))[None, :]
        kn = k_norm_w[i][nperm][None, :]
        cw = jnp.pad(conv_w[i], ((0, 8 - SHORT_CONV), (0, 0)))
        gnw = gdn_norm_w[i][None, :]
        wpa = w_proj_a[i].astype(BF16)
        wpb = w_proj_b[i][operm, :].astype(BF16)
        wout = w_out[i].astype(BF16)

        outs_c = _proj_call(ctx, scale_c, shift_c, normw, w_packed, gvec, qn, kn, gmat, cos_c, sin_c,
                            with_rest=update_ctx, tm=tm_c)
        qkv_c, gates_c, kv_c = outs_c[:3]
        qkv_c = _conv_call(qkv_c, cw)
        prep_c = _prep_call(qkv_c, gates_c, tc=tc_c)
        of_c, ob_c, s_ctx = _chain_call(*prep_c, s0, tc=tc_c)

        qkv_l, gates_l, kv_l, za_l, q_l, zb_l, ga_l, gb_l = _proj_call(
            x, scale_l, shift_l, normw, w_packed, gvec, qn, kn, gmat, cos_l, sin_l, with_rest=True, tm=tm_l)
        qkv_l = _conv_call(qkv_l, cw)
        prep_l = _prep_call(qkv_l, gates_l, tc=tc_l)
        of_l, ob_l, _ = _chain_call(*prep_l, s_ctx, tc=tc_l)
        yb_l = _attn_call(sink[i], q_l, kv_l, kv_c, zb_l, local=True)
        x_new = _merge_call(x, gate_l, of_l, ob_l, za_l, yb_l, ga_l, gb_l, gnw, wpa, wpb, wout, tm=tm_l)

        if update_ctx:
            za_c, q_c, zb_c, ga_c, gb_c = outs_c[3:]
            yb_c = _attn_call(sink[i], q_c, kv_c, kv_c, zb_c, local=False)
            ctx = _merge_call(ctx, gate_c, of_c, ob_c, za_c, yb_c, ga_c, gb_c, gnw, wpa, wpb, wout, tm=tm_c)
        x = x_new
    return x
```

```python
import functools
import math

import numpy as np
import jax
import jax.numpy as jnp
from jax import lax
from jax.experimental import pallas as pl
from jax.experimental.pallas import tpu as pltpu

F32 = jnp.float32
BF16 = jnp.bfloat16

D_MODEL = 1024
DEPTH = 2
GRID_W = 64
EPS = 1e-6
NEG_INF = -1e30

GDN_HEADS = 4
GDN_DK = 128
GDN_DV = 128
GDN_QKV = GDN_HEADS * (2 * GDN_DK + GDN_DV)
GDN_WIDTH = GDN_HEADS * GDN_DV
SHORT_CONV = 5
CHUNK = 64

ATT_HEADS = 8
ATT_KV_HEADS = 2
ATT_GROUP = ATT_HEADS // ATT_KV_HEADS
ATT_HD = 64
ATT_WIDTH = ATT_HEADS * ATT_HD
ATT_KV_WIDTH = ATT_KV_HEADS * ATT_HD
ATT_BLOCK = 128
ROPE_BASE = 10000.0
AXIS_DIM = ATT_HD // 2
ROPE_F = AXIS_DIM // 2

N_STATE = GDN_QKV + 4 * GDN_HEADS + 2 * ATT_KV_WIDTH

LANES = 128
P_QKV = 0
P_GATE = GDN_QKV
P_K = P_GATE + LANES
P_V = P_K + ATT_KV_WIDTH
P_STATE_END = P_V + ATT_KV_WIDTH
P_ZA = P_STATE_END
P_Q = P_ZA + GDN_WIDTH
P_ZB = P_Q + ATT_WIDTH
P_GA = P_ZB + ATT_WIDTH
P_GB = P_GA + D_MODEL
P_END = P_GB + D_MODEL

VMEM_LIMIT = 56 * 1024 * 1024


def _cparams(sem):
    return pltpu.CompilerParams(dimension_semantics=sem, vmem_limit_bytes=VMEM_LIMIT)


def _k_perm():
    idx = np.zeros(ATT_KV_WIDTH, np.int32)
    for slab in range(4):
        for kvh in range(ATT_KV_HEADS):
            for f in range(ROPE_F):
                idx[slab * 32 + kvh * 16 + f] = kvh * ATT_HD + slab * ROPE_F + f
    return idx


def _q_perm():
    idx = np.zeros(ATT_WIDTH, np.int32)
    for g in range(ATT_GROUP):
        for slab in range(4):
            for kvh in range(ATT_KV_HEADS):
                for f in range(ROPE_F):
                    idx[g * 128 + slab * 32 + kvh * 16 + f] = (kvh * ATT_GROUP + g) * ATT_HD + slab * ROPE_F + f
    return idx


def _o_perm():
    idx = np.zeros(ATT_WIDTH, np.int32)
    for g in range(ATT_GROUP):
        for kvh in range(ATT_KV_HEADS):
            for d in range(ATT_HD):
                idx[g * 128 + kvh * ATT_HD + d] = (kvh * ATT_GROUP + g) * ATT_HD + d
    return idx


def _norm_perm():
    idx = np.zeros(LANES, np.int32)
    for slab in range(4):
        for kvh in range(ATT_KV_HEADS):
            for f in range(ROPE_F):
                idx[slab * 32 + kvh * 16 + f] = slab * ROPE_F + f
    return idx


def _pack_w_in(w_in):
    o = 0
    qkv = w_in[:, o:o + GDN_QKV]; o += GDN_QKV
    gates = w_in[:, o:o + 4 * GDN_HEADS]; o += 4 * GDN_HEADS
    kb = w_in[:, o:o + ATT_KV_WIDTH]; o += ATT_KV_WIDTH
    vb = w_in[:, o:o + ATT_KV_WIDTH]; o += ATT_KV_WIDTH
    za = w_in[:, o:o + GDN_WIDTH]; o += GDN_WIDTH
    qb = w_in[:, o:o + ATT_WIDTH]; o += ATT_WIDTH
    zb = w_in[:, o:o + ATT_WIDTH]; o += ATT_WIDTH
    ga = w_in[:, o:o + D_MODEL]; o += D_MODEL
    gb = w_in[:, o:o + D_MODEL]
    gates = jnp.pad(gates, ((0, 0), (0, LANES - 4 * GDN_HEADS)))
    packed = jnp.concatenate(
        [qkv, gates, kb[:, _k_perm()], vb, za, qb[:, _q_perm()], zb[:, _o_perm()], ga, gb], axis=1)
    return packed.astype(BF16)


def _rope_tables(seq_len):
    t = jnp.arange(seq_len, dtype=F32)
    pos_row = jnp.floor(t / GRID_W)
    pos_col = t - pos_row * GRID_W
    inv = ROPE_BASE ** (-jnp.arange(0, AXIS_DIM, 2, dtype=F32) / AXIS_DIM)
    ang_r = pos_row[:, None] * inv[None, :]
    ang_c = pos_col[:, None] * inv[None, :]

    def lay(a):
        return jnp.concatenate([a, a], axis=1)

    cos = jnp.concatenate([lay(jnp.cos(ang_r)), lay(jnp.cos(ang_r)), lay(jnp.cos(ang_c)), lay(jnp.cos(ang_c))], axis=1)
    sin = jnp.concatenate([-lay(jnp.sin(ang_r)), lay(jnp.sin(ang_r)), -lay(jnp.sin(ang_c)), lay(jnp.sin(ang_c))], axis=1)
    return cos, sin


def _mod_kernel(c_ref, w_ref, b_ref, o_ref):
    c = c_ref[...]
    s = c * jax.nn.sigmoid(c)
    o_ref[0] = jnp.dot(s.astype(BF16), w_ref[0].astype(BF16), preferred_element_type=F32) + b_ref[0]


def _mod_call(c_all, w_mod, b_mod):
    rows = c_all.shape[0]
    nblk = 3
    return pl.pallas_call(
        _mod_kernel,
        grid=(DEPTH, nblk),
        in_specs=[
            pl.BlockSpec((rows, D_MODEL), lambda l, j: (0, 0)),
            pl.BlockSpec((1, D_MODEL, D_MODEL), lambda l, j: (l, 0, j)),
            pl.BlockSpec((1, 1, D_MODEL), lambda l, j: (l, 0, j)),
        ],
        out_specs=pl.BlockSpec((1, rows, D_MODEL), lambda l, j: (l, 0, j)),
        out_shape=jax.ShapeDtypeStruct((DEPTH, rows, 3 * D_MODEL), F32),
        compiler_params=_cparams(("arbitrary", "arbitrary")),
        name="mod",
    )(c_all, w_mod, b_mod.reshape(DEPTH, 1, 3 * D_MODEL))


def _swap32(t):
    lane = lax.broadcasted_iota(jnp.int32, t.shape, 1)
    even = (lane // 32) % 2 == 0
    return jnp.where(even, pltpu.roll(t, 96, 1), pltpu.roll(t, 32, 1))


def _headnorm_rope(t, nw, gmat, cos, sin):
    sq = t * t
    hi = sq.astype(BF16)
    lo = (sq - hi.astype(F32)).astype(BF16)
    ss = jnp.dot(hi, gmat, preferred_element_type=F32) + jnp.dot(lo, gmat, preferred_element_type=F32)
    tn = t * lax.rsqrt(ss * (1.0 / ATT_HD) + EPS) * nw
    return tn * cos + _swap32(tn) * sin


def _softplus(x):
    return jnp.maximum(x, 0.0) + jnp.log1p(jnp.exp(-jnp.abs(x)))


def _proj_kernel(x_ref, scale_ref, shift_ref, normw_ref, w_ref, gvec_ref, qn_ref, kn_ref, gmat_ref,
                 cos_ref, sin_ref, *out_refs, with_rest):
    if with_rest:
        qkv_ref, gates_ref, kv_ref, za_ref, q_ref, zb_ref, ga_ref, gb_ref = out_refs
    else:
        qkv_ref, gates_ref, kv_ref = out_refs
    x = x_ref[0]
    ms = jnp.mean(x * x, axis=-1, keepdims=True)
    a = normw_ref[...] * (1.0 + scale_ref[0])
    h = (x * lax.rsqrt(ms + EPS) * a + shift_ref[0]).astype(BF16)

    def proj(lo, width):
        return jnp.dot(h, w_ref[:, lo:lo + width], preferred_element_type=F32)

    for c in range(GDN_QKV // 512):
        qkv_ref[0, :, c * 512:(c + 1) * 512] = proj(P_QKV + c * 512, 512).astype(BF16)

    g = proj(P_GATE, LANES)
    lane = lax.broadcasted_iota(jnp.int32, g.shape, 1)
    beta = jax.nn.sigmoid(g)
    dec = -gvec_ref[0:1, :] * _softplus(g + gvec_ref[1:2, :])
    gates_ref[0] = jnp.where(lane < 2 * GDN_HEADS, beta, dec)

    gmat = gmat_ref[...]
    cos = cos_ref[...]
    sin = sin_ref[...]
    k = _headnorm_rope(proj(P_K, LANES), kn_ref[...], gmat, cos, sin)
    kv_ref[0, :, 0:LANES] = k.astype(BF16)
    kv_ref[0, :, LANES:2 * LANES] = proj(P_V, LANES).astype(BF16)

    if with_rest:
        z = proj(P_ZA, GDN_WIDTH)
        za_ref[0] = (z * jax.nn.sigmoid(z)).astype(BF16)
        for gi in range(ATT_GROUP):
            qg = _headnorm_rope(proj(P_Q + gi * LANES, LANES), qn_ref[...], gmat, cos, sin)
            q_ref[0, :, gi * LANES:(gi + 1) * LANES] = qg.astype(BF16)
        z = proj(P_ZB, ATT_WIDTH)
        zb_ref[0] = (z * jax.nn.sigmoid(z)).astype(BF16)
        for c in range(D_MODEL // 512):
            ga_ref[0, :, c * 512:(c + 1) * 512] = jax.nn.sigmoid(proj(P_GA + c * 512, 512)).astype(BF16)
            gb_ref[0, :, c * 512:(c + 1) * 512] = jax.nn.sigmoid(proj(P_GB + c * 512, 512)).astype(BF16)


def _proj_call(x, scale, shift, normw, w_packed, gvec, qn, kn, gmat, cos, sin, *, with_rest, tm):
    b, l, _ = x.shape
    nw = P_END if with_rest else P_STATE_END
    per_batch_mod = scale.shape[0] == b and b > 1
    mod_map = (lambda i, j: (i, 0, 0)) if per_batch_mod else (lambda i, j: (0, 0, 0))
    const2 = lambda i, j: (0, 0)
    row_map = lambda i, j: (i, j, 0)
    in_specs = [
        pl.BlockSpec((1, tm, D_MODEL), row_map),
        pl.BlockSpec((1, 1, D_MODEL), mod_map),
        pl.BlockSpec((1, 1, D_MODEL), mod_map),
        pl.BlockSpec((1, D_MODEL), const2),
        pl.BlockSpec((D_MODEL, nw), const2),
        pl.BlockSpec((8, LANES), const2),
        pl.BlockSpec((1, LANES), const2),
        pl.BlockSpec((1, LANES), const2),
        pl.BlockSpec((LANES, LANES), const2),
        pl.BlockSpec((tm, LANES), lambda i, j: (j, 0)),
        pl.BlockSpec((tm, LANES), lambda i, j: (j, 0)),
    ]
    widths = [(GDN_QKV, BF16), (LANES, F32), (2 * LANES, BF16)]
    if with_rest:
        widths += [(GDN_WIDTH, BF16), (ATT_WIDTH, BF16), (ATT_WIDTH, BF16), (D_MODEL, BF16), (D_MODEL, BF16)]
    out_specs = [pl.BlockSpec((1, tm, w), row_map) for w, _ in widths]
    out_shape = [jax.ShapeDtypeStruct((b, l, w), dt) for w, dt in widths]
    return pl.pallas_call(
        functools.partial(_proj_kernel, with_rest=with_rest),
        grid=(b, l // tm),
        in_specs=in_specs,
        out_specs=out_specs,
        out_shape=out_shape,
        compiler_params=_cparams(("arbitrary", "arbitrary")),
        name="proj_full" if with_rest else "proj_state",
    )(x, scale, shift, normw, w_packed[:, :nw], gvec, qn, kn, gmat, cos, sin)


def _conv_kernel(x_ref, w_ref, o_ref):
    j = pl.program_id(1)
    l = x_ref.shape[1]
    x = x_ref[0].astype(F32)
    pad = jnp.zeros((8, LANES), F32)
    xp = jnp.concatenate([pad, x, pad], axis=0)
    w = w_ref[...]
    half = SHORT_CONV // 2
    y = jnp.zeros((l, LANES), F32)
    for tap in range(SHORT_CONV):
        y = y + xp[8 - half + tap:8 - half + tap + l, :] * w[tap:tap + 1, :]
    y = y * jax.nn.sigmoid(y)
    ss = jnp.sum(y * y, axis=-1, keepdims=True)
    yn = y * lax.rsqrt(ss + EPS)
    kind = j // GDN_HEADS
    qscale = jnp.where(kind == 0, GDN_DK ** -0.5, 1.0).astype(F32)
    o_ref[0] = jnp.where(kind < 2, yn * qscale, y).astype(BF16)


def _conv_call(qkv_raw, conv_w):
    b, l, _ = qkv_raw.shape
    nblk = GDN_QKV // LANES
    return pl.pallas_call(
        _conv_kernel,
        grid=(b, nblk),
        in_specs=[
            pl.BlockSpec((1, l, LANES), lambda i, j: (i, 0, j)),
            pl.BlockSpec((8, LANES), lambda i, j: (0, j)),
        ],
        out_specs=pl.BlockSpec((1, l, LANES), lambda i, j: (i, 0, j)),
        out_shape=jax.ShapeDtypeStruct((b, l, GDN_QKV), BF16),
        compiler_params=_cparams(("arbitrary", "arbitrary")),
        name="conv",
    )(qkv_raw, conv_w)


def _blockdiag(y, isf):
    zero = jnp.zeros_like(y)
    return jnp.concatenate([jnp.where(isf, y, zero), jnp.where(isf, zero, y)], axis=0)


def _prep_kernel(qkv_ref, gates_ref, u_ref, w_ref, kdt_ref, qd_ref, qk_ref, eg_ref,
                 a_scr, t_scr, rhs_scr, *, tc):
    H = GDN_HEADS
    C = CHUNK
    G = gates_ref[0]
    gT = G.T[2 * H:4 * H, :]
    lane_t = lax.broadcasted_iota(jnp.int32, gT.shape, 1)
    pos = lane_t % C
    pre = gT
    suf = gT
    for s in (1, 2, 4, 8, 16, 32):
        pre = pre + jnp.where(pos >= s, pltpu.roll(pre, s, 1), 0.0)
        suf = suf + jnp.where(pos < C - s, pltpu.roll(suf, tc - s, 1), 0.0)
    row8 = lax.broadcasted_iota(jnp.int32, gT.shape, 0)
    gcT = jnp.where(row8 < H, pre, suf)
    gc = jnp.concatenate([gcT, jnp.zeros((LANES - 2 * H, tc), F32)], axis=0).T

    lane = lax.broadcasted_iota(jnp.int32, (C, LANES), 1)
    ii = lax.broadcasted_iota(jnp.int32, (C, LANES), 0)
    jj = lane % C
    isf = lane < C
    isb = jnp.logical_not(isf)
    incl = jnp.logical_or(jnp.logical_and(isf, ii >= jj), jnp.logical_and(isb, ii <= jj))
    strict = jnp.logical_or(jnp.logical_and(isf, ii > jj), jnp.logical_and(isb, ii < jj))
    eye2 = (ii == jj).astype(F32)
    same = {kk: (ii // kk) == (jj // kk) for kk in (2, 4, 8, 16, 32, 64)}
    lane8 = lax.broadcasted_iota(jnp.int32, (1, LANES), 1)

    def colb(arr, r0, c):
        return jnp.broadcast_to(arr[r0:r0 + C, c:c + 1], (C, LANES))

    for s in range(tc // C):
        r0 = s * C
        glrow = jnp.where(lane8 < H, gc[r0 + C - 1:r0 + C, :], gc[r0:r0 + 1, :])
        eg_ref[0, r0:r0 + C, :] = jnp.broadcast_to(jnp.exp(glrow), (C, LANES))

    items = [(lt, h, halfsel) for lt in range(tc // LANES) for h in range(H) for halfsel in range(2)]
    gts = {}
    for lt in range(tc // LANES):
        gt = gcT[:, lt * LANES:(lt + 1) * LANES]
        gts[lt] = (gt, pltpu.roll(gt, C, 1))

    kd_parts = {}
    for idx, (lt, h, halfsel) in enumerate(items):
        gt, gt_r = gts[lt]
        r0 = (2 * lt + halfsel) * C
        if halfsel == 0:
            row_f, row_b = gt[h:h + 1, :], gt_r[H + h:H + h + 1, :]
        else:
            row_f, row_b = gt_r[h:h + 1, :], gt[H + h:H + h + 1, :]
        gc_row2 = jnp.broadcast_to(jnp.where(lane8 < C, row_f, row_b), (C, LANES))
        gcf = colb(gc, r0, h)
        gcb = colb(gc, r0, H + h)
        dec = jnp.where(incl, jnp.exp(jnp.where(incl, jnp.where(isf, gcf, gcb) - gc_row2, 0.0)), 0.0)
        bf = colb(G, r0, h)
        bb = colb(G, r0, H + h)
        qt = qkv_ref[0, r0:r0 + C, h * LANES:(h + 1) * LANES]
        kt = qkv_ref[0, r0:r0 + C, (H + h) * LANES:(H + h + 1) * LANES]
        vt = qkv_ref[0, r0:r0 + C, (2 * H + h) * LANES:(2 * H + h + 1) * LANES]
        kq = lax.dot_general(jnp.concatenate([kt, qt], axis=0), jnp.concatenate([kt, kt], axis=0),
                             (((1,), (1,)), ((), ())), preferred_element_type=F32)
        a2 = jnp.where(strict, jnp.where(isf, bf, bb) * kq[0:C] * dec, 0.0)
        a_scr[idx] = a2
        t_scr[idx] = eye2 - jnp.where(same[2], a2, 0.0)
        qk_ref[0, r0:r0 + C, h * LANES:(h + 1) * LANES] = (kq[C:2 * C] * dec).astype(BF16)

        kf = kt.astype(F32)
        vf = vt.astype(F32)
        qf = qt.astype(F32)
        egf = jnp.exp(gcf)
        egb = jnp.exp(gcb)
        z2 = jnp.zeros((C, 2 * LANES), F32)
        rf = jnp.concatenate([vf * bf, kf * (bf * egf), z2], axis=1)
        rb = jnp.concatenate([z2, vf * bb, kf * (bb * egb)], axis=1)
        rhs_scr[idx] = jnp.concatenate([rf, rb], axis=0).astype(BF16)
        cf = h * LANES
        cb = (H + h) * LANES
        qd_ref[0, r0:r0 + C, cf:cf + LANES] = (qf * egf).astype(BF16)
        qd_ref[0, r0:r0 + C, cb:cb + LANES] = (qf * egb).astype(BF16)
        glf = jnp.broadcast_to(gc[r0 + C - 1:r0 + C, h:h + 1], (C, LANES))
        glb = jnp.broadcast_to(gc[r0:r0 + 1, H + h:H + h + 1], (C, LANES))
        kd_parts[(lt, h, halfsel)] = (kf * jnp.exp(glf - gcf), kf * jnp.exp(glb - gcb))
        if halfsel == 1:
            for d in range(2):
                kdt = jnp.concatenate([kd_parts[(lt, h, 0)][d], kd_parts[(lt, h, 1)][d]], axis=0).T
                kdt_ref[0, (d * H + h) * LANES:(d * H + h + 1) * LANES, lt * LANES:(lt + 1) * LANES] = kdt.astype(BF16)

    for kk in (2, 4, 8, 16, 32):
        emask = jnp.logical_and(same[2 * kk], jnp.logical_not(same[kk]))
        ps = []
        for idx in range(len(items)):
            e = jnp.where(emask, a_scr[idx], 0.0).astype(BF16)
            ps.append(jnp.dot(e, _blockdiag(t_scr[idx].astype(BF16), isf), preferred_element_type=F32))
        for idx in range(len(items)):
            t2 = t_scr[idx]
            t_scr[idx] = t2 - jnp.dot(t2.astype(BF16), _blockdiag(ps[idx].astype(BF16), isf),
                                      preferred_element_type=F32)

    for idx, (lt, h, halfsel) in enumerate(items):
        r0 = (2 * lt + halfsel) * C
        uw = jnp.dot(t_scr[idx].astype(BF16), rhs_scr[idx], preferred_element_type=F32)
        cf = h * LANES
        cb = (H + h) * LANES
        u_ref[0, r0:r0 + C, cf:cf + LANES] = uw[:, 0:LANES].astype(BF16)
        w_ref[0, r0:r0 + C, cf:cf + LANES] = uw[:, LANES:2 * LANES].astype(BF16)
        u_ref[0, r0:r0 + C, cb:cb + LANES] = uw[:, 2 * LANES:3 * LANES].astype(BF16)
        w_ref[0, r0:r0 + C, cb:cb + LANES] = uw[:, 3 * LANES:4 * LANES].astype(BF16)


def _prep_call(qkv, gates, *, tc):
    b, l, _ = qkv.shape
    row_map = lambda i, j: (i, j, 0)
    wide = 2 * GDN_HEADS * LANES
    n_items = (tc // CHUNK) * GDN_HEADS
    return pl.pallas_call(
        functools.partial(_prep_kernel, tc=tc),
        grid=(b, l // tc),
        in_specs=[pl.BlockSpec((1, tc, GDN_QKV), row_map), pl.BlockSpec((1, tc, LANES), row_map)],
        out_specs=[
            pl.BlockSpec((1, tc, wide), row_map),
            pl.BlockSpec((1, tc, wide), row_map),
            pl.BlockSpec((1, wide, tc), lambda i, j: (i, 0, j)),
            pl.BlockSpec((1, tc, wide), row_map),
            pl.BlockSpec((1, tc, GDN_HEADS * LANES), row_map),
            pl.BlockSpec((1, tc, LANES), row_map),
        ],
        out_shape=[
            jax.ShapeDtypeStruct((b, l, wide), BF16),
            jax.ShapeDtypeStruct((b, l, wide), BF16),
            jax.ShapeDtypeStruct((b, wide, l), BF16),
            jax.ShapeDtypeStruct((b, l, wide), BF16),
            jax.ShapeDtypeStruct((b, l, GDN_HEADS * LANES), BF16),
            jax.ShapeDtypeStruct((b, l, LANES), F32),
        ],
        scratch_shapes=[
            pltpu.VMEM((n_items, CHUNK, LANES), F32),
            pltpu.VMEM((n_items, CHUNK, LANES), F32),
            pltpu.VMEM((n_items, 2 * CHUNK, 4 * LANES), BF16),
        ],
        compiler_params=_cparams(("arbitrary", "arbitrary")),
        name="gdn_prep",
    )(qkv, gates)


def _chain_kernel(uf_ref, wf_ref, kdtf_ref, qdf_ref, qkf_ref, egf_ref,
                  ub_ref, wb_ref, kdtb_ref, qdb_ref, qkb_ref, egb_ref, s0_ref,
                  of_ref, ob_ref, sfin_ref, s_scr, *, tc):
    H = GDN_HEADS
    C = CHUNK
    i = pl.program_id(1)
    n = pl.num_programs(1)

    @pl.when(i == 0)
    def _():
        s_scr[...] = s0_ref[0]

    nch = tc // C
    zc = jnp.zeros((C, LANES), BF16)
    refs = ((uf_ref, wf_ref, kdtf_ref, qdf_ref, qkf_ref, egf_ref, of_ref),
            (ub_ref, wb_ref, kdtb_ref, qdb_ref, qkb_ref, egb_ref, ob_ref))
    chains = [(d, h) for d in range(2) for h in range(H)]
    for c in range(nch):
        rs = []
        for d, h in chains:
            u_ref, w_ref, kdt_ref, qd_ref, qk_ref, eg_ref, o_ref = refs[d]
            r0 = (c if d == 0 else nch - 1 - c) * C
            cs = h * LANES
            lhs = jnp.concatenate([w_ref[0, r0:r0 + C, cs:cs + LANES], qd_ref[0, r0:r0 + C, cs:cs + LANES]], axis=0)
            rs.append(jnp.dot(lhs, s_scr[d, h].astype(BF16), preferred_element_type=F32))
        vbs = []
        for (d, h), r in zip(chains, rs):
            u_ref, w_ref, kdt_ref, qd_ref, qk_ref, eg_ref, o_ref = refs[d]
            r0 = (c if d == 0 else nch - 1 - c) * C
            cs = h * LANES
            vbs.append((u_ref[0, r0:r0 + C, cs:cs + LANES].astype(F32) - r[0:C]).astype(BF16))
        for (d, h), vb in zip(chains, vbs):
            u_ref, w_ref, kdt_ref, qd_ref, qk_ref, eg_ref, o_ref = refs[d]
            cc = c if d == 0 else nch - 1 - c
            lt, par = divmod(cc, 2)
            cs = h * LANES
            v_par = jnp.concatenate([vb, zc], axis=0) if par == 0 else jnp.concatenate([zc, vb], axis=0)
            kdt = kdt_ref[0, cs:cs + LANES, lt * LANES:(lt + 1) * LANES]
            eg = jnp.broadcast_to(eg_ref[0, cc * C:cc * C + 1, d * H + h:d * H + h + 1], (LANES, LANES))
            s_scr[d, h] = s_scr[d, h] * eg + jnp.dot(kdt, v_par, preferred_element_type=F32)
        for (d, h), r, vb in zip(chains, rs, vbs):
            u_ref, w_ref, kdt_ref, qd_ref, qk_ref, eg_ref, o_ref = refs[d]
            r0 = (c if d == 0 else nch - 1 - c) * C
            cs = h * LANES
            v_dir = jnp.concatenate([vb, zc], axis=0) if d == 0 else jnp.concatenate([zc, vb], axis=0)
            o_ref[0, r0:r0 + C, cs:cs + LANES] = r[C:2 * C] + jnp.dot(
                qk_ref[0, r0:r0 + C, cs:cs + LANES], v_dir, preferred_element_type=F32)

    @pl.when(i == n - 1)
    def _():
        sfin_ref[0] = s_scr[...]


def _chain_call(u, w, kdt, qd, qk, eg, s0, *, tc):
    b, l, _ = u.shape
    n = l // tc
    hw = GDN_HEADS * LANES
    fwd = lambda i, j: (i, j, 0)
    bwd = lambda i, j: (i, n - 1 - j, 1)
    bwd0 = lambda i, j: (i, n - 1 - j, 0)
    in_specs = [
        pl.BlockSpec((1, tc, hw), fwd), pl.BlockSpec((1, tc, hw), fwd),
        pl.BlockSpec((1, hw, tc), lambda i, j: (i, 0, j)),
        pl.BlockSpec((1, tc, hw), fwd), pl.BlockSpec((1, tc, hw), fwd), pl.BlockSpec((1, tc, LANES), fwd),
        pl.BlockSpec((1, tc, hw), bwd), pl.BlockSpec((1, tc, hw), bwd),
        pl.BlockSpec((1, hw, tc), lambda i, j: (i, 1, n - 1 - j)),
        pl.BlockSpec((1, tc, hw), bwd), pl.BlockSpec((1, tc, hw), bwd0), pl.BlockSpec((1, tc, LANES), bwd0),
        pl.BlockSpec((1, 2, GDN_HEADS, LANES, LANES), lambda i, j: (i, 0, 0, 0, 0)),
    ]
    out_specs = [
        pl.BlockSpec((1, tc, hw), fwd),
        pl.BlockSpec((1, tc, hw), bwd0),
        pl.BlockSpec((1, 2, GDN_HEADS, LANES, LANES), lambda i, j: (i, 0, 0, 0, 0)),
    ]
    out_shape = [
        jax.ShapeDtypeStruct((b, l, hw), F32),
        jax.ShapeDtypeStruct((b, l, hw), F32),
        jax.ShapeDtypeStruct((b, 2, GDN_HEADS, LANES, LANES), F32),
    ]
    return pl.pallas_call(
        functools.partial(_chain_kernel, tc=tc),
        grid=(b, n),
        in_specs=in_specs,
        out_specs=out_specs,
        out_shape=out_shape,
        scratch_shapes=[pltpu.VMEM((2, GDN_HEADS, LANES, LANES), F32)],
        compiler_params=_cparams(("arbitrary", "arbitrary")),
        name="gdn_chain",
    )(u, w, kdt, qd, qk, eg, u, w, kdt, qd, qk, eg, s0)


def _attn_kernel(sink_ref, q_ref, kvp_ref, kvo_ref, kvn_ref, kvc_ref, zb_ref, o_ref, *, local):
    nb = pl.num_programs(1)
    n = pl.program_id(1)
    T = ATT_BLOCK
    q = q_ref[0]
    qs = jnp.concatenate([q[:, g * LANES:(g + 1) * LANES] for g in range(ATT_GROUP)], axis=0)
    lane = lax.broadcasted_iota(jnp.int32, (1, LANES), 1)
    rowi = lax.broadcasted_iota(jnp.int32, (ATT_GROUP * T, T), 0) % T
    colj = lax.broadcasted_iota(jnp.int32, (ATT_GROUP * T, T), 1)
    grp = lax.broadcasted_iota(jnp.int32, (ATT_GROUP * T, 1), 0) // T

    pieces = [(kvc_ref, None)]
    if local:
        pieces += [(kvp_ref, jnp.logical_and(colj >= rowi, n > 0)),
                   (kvo_ref, None),
                   (kvn_ref, jnp.logical_and(colj <= rowi, n < nb - 1))]

    acc = jnp.zeros((ATT_GROUP * T, LANES), F32)
    for kvh in range(ATT_KV_HEADS):
        kmask = ((lane // ROPE_F) % ATT_KV_HEADS == kvh)
        vmask = (lane // ATT_HD == kvh)
        sink = jnp.zeros((ATT_GROUP * T, 1), F32)
        for g in range(ATT_GROUP):
            sink = jnp.where(grp == g, sink_ref[kvh * ATT_GROUP + g], sink)
        scores = []
        m = sink
        for ref, valid in pieces:
            kv = ref[0]
            kh = jnp.where(kmask, kv[:, 0:LANES], jnp.zeros_like(kv[:, 0:LANES]))
            s = lax.dot_general(qs, kh, (((1,), (1,)), ((), ())), preferred_element_type=F32)
            if valid is not None:
                s = jnp.where(valid, s, NEG_INF)
            scores.append(s)
            m = jnp.maximum(m, jnp.max(s, axis=-1, keepdims=True))
        denom = jnp.exp(sink - m)
        o_h = jnp.zeros((ATT_GROUP * T, LANES), F32)
        for (ref, _), s in zip(pieces, scores):
            kv = ref[0]
            vh = jnp.where(vmask, kv[:, LANES:2 * LANES], jnp.zeros_like(kv[:, 0:LANES]))
            p = jnp.exp(s - m)
            denom = denom + jnp.sum(p, axis=-1, keepdims=True)
            o_h = o_h + jnp.dot(p.astype(BF16), vh, preferred_element_type=F32)
        acc = acc + o_h / denom
    for g in range(ATT_GROUP):
        og = acc[g * T:(g + 1) * T, :] * zb_ref[0, :, g * LANES:(g + 1) * LANES].astype(F32)
        o_ref[0, :, g * LANES:(g + 1) * LANES] = og.astype(BF16)


def _attn_call(sink, q, kv, kvc, zb, *, local):
    b, l, _ = q.shape
    nb = l // ATT_BLOCK
    lc = kvc.shape[1]
    blk = lambda f: pl.BlockSpec((1, ATT_BLOCK, 2 * LANES), f)
    return pl.pallas_call(
        functools.partial(_attn_kernel, local=local),
        grid=(b, nb),
        in_specs=[
            pl.BlockSpec(memory_space=pltpu.SMEM),
            pl.BlockSpec((1, ATT_BLOCK, ATT_WIDTH), lambda i, j: (i, j, 0)),
            blk(lambda i, j: (i, jnp.maximum(j - 1, 0), 0)),
            blk(lambda i, j: (i, j, 0)),
            blk(lambda i, j: (i, jnp.minimum(j + 1, nb - 1), 0)),
            pl.BlockSpec((1, lc, 2 * LANES), lambda i, j: (i, 0, 0)),
            pl.BlockSpec((1, ATT_BLOCK, ATT_WIDTH), lambda i, j: (i, j, 0)),
        ],
        out_specs=pl.BlockSpec((1, ATT_BLOCK, ATT_WIDTH), lambda i, j: (i, j, 0)),
        out_shape=jax.ShapeDtypeStruct((b, l, ATT_WIDTH), BF16),
        compiler_params=_cparams(("arbitrary", "arbitrary")),
        name="attn_local" if local else "attn_ctx",
    )(sink, q, kv, kv, kv, kvc, zb)


def _merge_kernel(x_ref, gate_ref, of_ref, ob_ref, za_ref, yb_ref, ga_ref, gb_ref, gnw_ref,
                  wpa_ref, wpb_ref, wout_ref, o_ref):
    gnw = gnw_ref[...]
    ya_parts = []
    for h in range(GDN_HEADS):
        sl = slice(h * LANES, (h + 1) * LANES)
        o = of_ref[0, :, sl] + ob_ref[0, :, sl]
        on = o * lax.rsqrt(jnp.mean(o * o, axis=-1, keepdims=True) + EPS) * gnw
        ya_parts.append((on * za_ref[0, :, sl].astype(F32)).astype(BF16))
    ya = jnp.concatenate(ya_parts, axis=1)
    pa = jnp.dot(ya, wpa_ref[...], preferred_element_type=F32)
    pb = jnp.dot(yb_ref[0], wpb_ref[...], preferred_element_type=F32)
    y = ga_ref[0].astype(F32) * pa + gb_ref[0].astype(F32) * pb
    out = jnp.dot(y.astype(BF16), wout_ref[...], preferred_element_type=F32)
    o_ref[0] = x_ref[0] + gate_ref[0] * out


def _merge_call(x, gate, of, ob, za, yb, ga, gb, gnw, wpa, wpb, wout, *, tm):
    b, l, _ = x.shape
    per_batch_mod = gate.shape[0] == b and b > 1
    mod_map = (lambda i, j: (i, 0, 0)) if per_batch_mod else (lambda i, j: (0, 0, 0))
    row = lambda w: pl.BlockSpec((1, tm, w), lambda i, j: (i, j, 0))
    const2 = lambda i, j: (0, 0)
    return pl.pallas_call(
        _merge_kernel,
        grid=(b, l // tm),
        in_specs=[
            row(D_MODEL), pl.BlockSpec((1, 1, D_MODEL), mod_map),
            row(GDN_WIDTH), row(GDN_WIDTH), row(GDN_WIDTH), row(ATT_WIDTH), row(D_MODEL), row(D_MODEL),
            pl.BlockSpec((1, LANES), const2),
            pl.BlockSpec((GDN_WIDTH, D_MODEL), const2),
            pl.BlockSpec((ATT_WIDTH, D_MODEL), const2),
            pl.BlockSpec((D_MODEL, D_MODEL), const2),
        ],
        out_specs=row(D_MODEL),
        out_shape=jax.ShapeDtypeStruct((b, l, D_MODEL), F32),
        compiler_params=_cparams(("arbitrary", "arbitrary")),
        name="merge",
    )(x, gate, of, ob, za, yb, ga, gb, gnw, wpa, wpb, wout)


def _pick_tile(l, pref):
    t = min(pref, l)
    while l % t:
        t //= 2
    return t


def kernel(x, c, ctx, c_ctx, norm_w, w_mod, b_mod, w_in, conv_w, a_log, dt_bias, gdn_norm_w,
           q_norm_w, k_norm_w, sink, w_proj_a, w_proj_b, w_out):
    b, l, _ = x.shape
    lc = ctx.shape[1]
    assert l % ATT_BLOCK == 0 and lc % ATT_BLOCK == 0 and x.shape[2] == D_MODEL

    rows = ((b + 1 + 7) // 8) * 8
    c_all = jnp.concatenate([c, c_ctx[None, :], jnp.zeros((rows - b - 1, D_MODEL), F32)], axis=0)
    mod = _mod_call(c_all, w_mod, b_mod)

    cos_l, sin_l = _rope_tables(l)
    cos_c = jnp.ones((lc, LANES), F32)
    sin_c = jnp.zeros((lc, LANES), F32)
    grp = (np.arange(LANES) // ROPE_F) % ATT_KV_HEADS
    gmat = jnp.asarray((grp[:, None] == grp[None, :]).astype(np.float32), dtype=BF16)
    nperm = _norm_perm()
    operm = _o_perm()
    s0 = jnp.zeros((b, 2, GDN_HEADS, LANES, LANES), F32)
    tm_l = _pick_tile(l, 256)
    tm_c = _pick_tile(lc, 256)
    tc_l = _pick_tile(l, 256)
    tc_c = _pick_tile(lc, 256)

    for i in range(DEPTH):
        update_ctx = i < DEPTH - 1
        shift, scale, gate = (mod[i, :, k * D_MODEL:(k + 1) * D_MODEL] for k in range(3))
        shift_l, scale_l, gate_l = (t[:b, None, :] for t in (shift, scale, gate))
        shift_c, scale_c, gate_c = (t[b:b + 1, None, :] for t in (shift, scale, gate))
        w_packed = _pack_w_in(w_in[i])
        normw = norm_w[i][None, :]
        gvec = jnp.zeros((8, LANES), F32)
        gvec = gvec.at[0, 2 * GDN_HEADS:4 * GDN_HEADS].set(jnp.exp(a_log[i].astype(F32)).reshape(-1))
        gvec = gvec.at[1, 2 * GDN_HEADS:4 * GDN_HEADS].set(dt_bias[i].astype(F32).reshape(-1))
        qn = (q_norm_w[i][nperm] * (ATT_HD ** -0.5))[None, :]
        kn = k_norm_w[i][nperm][None, :]
        cw = jnp.pad(conv_w[i], ((0, 8 - SHORT_CONV), (0, 0)))
        gnw = gdn_norm_w[i][None, :]
        wpa = w_proj_a[i].astype(BF16)
        wpb = w_proj_b[i][operm, :].astype(BF16)
        wout = w_out[i].astype(BF16)

        outs_c = _proj_call(ctx, scale_c, shift_c, normw, w_packed, gvec, qn, kn, gmat, cos_c, sin_c,
                            with_rest=update_ctx, tm=tm_c)
        qkv_c, gates_c, kv_c = outs_c[:3]
        qkv_c = _conv_call(qkv_c, cw)
        prep_c = _prep_call(qkv_c, gates_c, tc=tc_c)
        of_c, ob_c, s_ctx = _chain_call(*prep_c, s0, tc=tc_c)

        qkv_l, gates_l, kv_l, za_l, q_l, zb_l, ga_l, gb_l = _proj_call(
            x, scale_l, shift_l, normw, w_packed, gvec, qn, kn, gmat, cos_l, sin_l, with_rest=True, tm=tm_l)
        qkv_l = _conv_call(qkv_l, cw)
        prep_l = _prep_call(qkv_l, gates_l, tc=tc_l)
        of_l, ob_l, _ = _chain_call(*prep_l, s_ctx, tc=tc_l)
        yb_l = _attn_call(sink[i], q_l, kv_l, kv_c, zb_l, local=True)
        x_new = _merge_call(x, gate_l, of_l, ob_l, za_l, yb_l, ga_l, gb_l, gnw, wpa, wpb, wout, tm=tm_l)

        if update_ctx:
            za_c, q_c, zb_c, ga_c, gb_c = outs_c[3:]
            yb_c = _attn_call(sink[i], q_c, kv_c, kv_c, zb_c, local=False)
            ctx = _merge_call(ctx, gate_c, of_c, ob_c, za_c, yb_c, ga_c, gb_c, gnw, wpa, wpb, wout, tm=tm_c)
        x = x_new
    return x
```

```python
import functools
import math

import numpy as np
import jax
import jax.numpy as jnp
from jax import lax
from jax.experimental import pallas as pl
from jax.experimental.pallas import tpu as pltpu

F32 = jnp.float32
BF16 = jnp.bfloat16

D_MODEL = 1024
DEPTH = 2
GRID_W = 64
EPS = 1e-6
NEG_INF = -1e30

GDN_HEADS = 4
GDN_DK = 128
GDN_DV = 128
GDN_QKV = GDN_HEADS * (2 * GDN_DK + GDN_DV)
GDN_WIDTH = GDN_HEADS * GDN_DV
SHORT_CONV = 5
CHUNK = 64

ATT_HEADS = 8
ATT_KV_HEADS = 2
ATT_GROUP = ATT_HEADS // ATT_KV_HEADS
ATT_HD = 64
ATT_WIDTH = ATT_HEADS * ATT_HD
ATT_KV_WIDTH = ATT_KV_HEADS * ATT_HD
ATT_BLOCK = 128
ROPE_BASE = 10000.0
AXIS_DIM = ATT_HD // 2
ROPE_F = AXIS_DIM // 2

N_STATE = GDN_QKV + 4 * GDN_HEADS + 2 * ATT_KV_WIDTH

LANES = 128
P_QKV = 0
P_GATE = GDN_QKV
P_K = P_GATE + LANES
P_V = P_K + ATT_KV_WIDTH
P_STATE_END = P_V + ATT_KV_WIDTH
P_ZA = P_STATE_END
P_Q = P_ZA + GDN_WIDTH
P_ZB = P_Q + ATT_WIDTH
P_GA = P_ZB + ATT_WIDTH
P_GB = P_GA + D_MODEL
P_END = P_GB + D_MODEL

VMEM_LIMIT = 56 * 1024 * 1024


def _cparams(sem):
    return pltpu.CompilerParams(dimension_semantics=sem, vmem_limit_bytes=VMEM_LIMIT)


def _k_cols(w):
    lead = w.shape[:-1]
    w = w.reshape(lead + (ATT_KV_HEADS, 4, ROPE_F))
    return jnp.swapaxes(w, -3, -2).reshape(lead + (ATT_KV_WIDTH,))


def _q_cols(w):
    lead = w.shape[:-1]
    w = w.reshape(lead + (ATT_KV_HEADS, ATT_GROUP, 4, ROPE_F))
    return jnp.moveaxis(w, -4, -2).reshape(lead + (ATT_WIDTH,))


def _o_cols(w):
    lead = w.shape[:-1]
    w = w.reshape(lead + (ATT_KV_HEADS, ATT_GROUP, ATT_HD))
    return jnp.swapaxes(w, -3, -2).reshape(lead + (ATT_WIDTH,))


def _norm_lanes(w):
    w = w.reshape(4, 1, ROPE_F)
    return jnp.broadcast_to(w, (4, ATT_KV_HEADS, ROPE_F)).reshape(LANES)


def _pack_w_in(w_in):
    w_in = w_in.astype(BF16)
    o = 0
    qkv = w_in[..., o:o + GDN_QKV]; o += GDN_QKV
    gates = w_in[..., o:o + 4 * GDN_HEADS]; o += 4 * GDN_HEADS
    kb = w_in[..., o:o + ATT_KV_WIDTH]; o += ATT_KV_WIDTH
    vb = w_in[..., o:o + ATT_KV_WIDTH]; o += ATT_KV_WIDTH
    za = w_in[..., o:o + GDN_WIDTH]; o += GDN_WIDTH
    qb = w_in[..., o:o + ATT_WIDTH]; o += ATT_WIDTH
    zb = w_in[..., o:o + ATT_WIDTH]; o += ATT_WIDTH
    ga = w_in[..., o:o + D_MODEL]; o += D_MODEL
    gb = w_in[..., o:o + D_MODEL]
    gates = jnp.concatenate([gates, jnp.zeros(gates.shape[:-1] + (LANES - 4 * GDN_HEADS,), BF16)], axis=-1)
    return jnp.concatenate(
        [qkv, gates, _k_cols(kb), vb, za, _q_cols(qb), _o_cols(zb), ga, gb], axis=-1)


def _rope_tables(seq_len):
    t = jnp.arange(seq_len, dtype=F32)
    pos_row = jnp.floor(t / GRID_W)
    pos_col = t - pos_row * GRID_W
    inv = ROPE_BASE ** (-jnp.arange(0, AXIS_DIM, 2, dtype=F32) / AXIS_DIM)
    ang_r = pos_row[:, None] * inv[None, :]
    ang_c = pos_col[:, None] * inv[None, :]

    def lay(a):
        return jnp.concatenate([a, a], axis=1)

    cos = jnp.concatenate([lay(jnp.cos(ang_r)), lay(jnp.cos(ang_r)), lay(jnp.cos(ang_c)), lay(jnp.cos(ang_c))], axis=1)
    sin = jnp.concatenate([-lay(jnp.sin(ang_r)), lay(jnp.sin(ang_r)), -lay(jnp.sin(ang_c)), lay(jnp.sin(ang_c))], axis=1)
    return cos, sin


def _mod_kernel(c_ref, w_ref, b_ref, o_ref):
    c = c_ref[...]
    s = c * jax.nn.sigmoid(c)
    o_ref[0] = jnp.dot(s.astype(BF16), w_ref[0].astype(BF16), preferred_element_type=F32) + b_ref[0]


def _mod_call(c_all, w_mod, b_mod):
    rows = c_all.shape[0]
    nblk = 3
    return pl.pallas_call(
        _mod_kernel,
        grid=(DEPTH, nblk),
        in_specs=[
            pl.BlockSpec((rows, D_MODEL), lambda l, j: (0, 0)),
            pl.BlockSpec((1, D_MODEL, D_MODEL), lambda l, j: (l, 0, j)),
            pl.BlockSpec((1, 1, D_MODEL), lambda l, j: (l, 0, j)),
        ],
        out_specs=pl.BlockSpec((1, rows, D_MODEL), lambda l, j: (l, 0, j)),
        out_shape=jax.ShapeDtypeStruct((DEPTH, rows, 3 * D_MODEL), F32),
        compiler_params=_cparams(("arbitrary", "arbitrary")),
        name="mod",
    )(c_all, w_mod, b_mod.reshape(DEPTH, 1, 3 * D_MODEL))


def _swap32(t):
    lane = lax.broadcasted_iota(jnp.int32, t.shape, 1)
    even = (lane // 32) % 2 == 0
    return jnp.where(even, pltpu.roll(t, 96, 1), pltpu.roll(t, 32, 1))


def _headnorm_rope(parts, nws, gmat, cos, sin):
    t = jnp.concatenate(parts, axis=0) if len(parts) > 1 else parts[0]
    ss = jnp.dot((t * t).astype(BF16), gmat, preferred_element_type=F32)
    inv = lax.rsqrt(ss * (1.0 / ATT_HD) + EPS)
    tm = parts[0].shape[0]
    outs = []
    for i, nw in enumerate(nws):
        tn = parts[i] * inv[i * tm:(i + 1) * tm] * nw
        outs.append(tn * cos + _swap32(tn) * sin)
    return outs


def _softplus(x):
    return jnp.maximum(x, 0.0) + jnp.log1p(jnp.exp(-jnp.abs(x)))


def _proj_kernel(x_ref, scale_ref, shift_ref, normw_ref, w_ref, gvec_ref, qn_ref, kn_ref, gmat_ref,
                 cos_ref, sin_ref, *out_refs, with_rest):
    if with_rest:
        qkv_ref, gates_ref, kv_ref, za_ref, q_ref, zb_ref, ga_ref, gb_ref = out_refs
    else:
        qkv_ref, gates_ref, kv_ref = out_refs
    x = x_ref[0]
    ms = jnp.mean(x * x, axis=-1, keepdims=True)
    a = normw_ref[...] * (1.0 + scale_ref[0])
    h = (x * lax.rsqrt(ms + EPS) * a + shift_ref[0]).astype(BF16)

    def proj(lo, width):
        return jnp.dot(h, w_ref[:, lo:lo + width], preferred_element_type=F32)

    for c in range(GDN_QKV // 512):
        qkv_ref[0, :, c * 512:(c + 1) * 512] = proj(P_QKV + c * 512, 512).astype(BF16)

    g = proj(P_GATE, LANES)
    lane = lax.broadcasted_iota(jnp.int32, g.shape, 1)
    beta = jax.nn.sigmoid(g)
    dec = -gvec_ref[0:1, :] * _softplus(g + gvec_ref[1:2, :])
    gates_ref[0] = jnp.where(lane < 2 * GDN_HEADS, beta, dec)

    gmat = gmat_ref[...]
    cos = cos_ref[...]
    sin = sin_ref[...]
    kv_ref[0, :, LANES:2 * LANES] = proj(P_V, LANES).astype(BF16)
    parts = [proj(P_K, LANES)]
    nws = [kn_ref[...]]
    if with_rest:
        qraw = proj(P_Q, ATT_WIDTH)
        parts += [qraw[:, gi * LANES:(gi + 1) * LANES] for gi in range(ATT_GROUP)]
        nws += [qn_ref[...]] * ATT_GROUP
    normed = _headnorm_rope(parts, nws, gmat, cos, sin)
    kv_ref[0, :, 0:LANES] = normed[0].astype(BF16)

    if with_rest:
        for gi in range(ATT_GROUP):
            q_ref[0, :, gi * LANES:(gi + 1) * LANES] = normed[1 + gi].astype(BF16)
        z = proj(P_ZA, GDN_WIDTH)
        za_ref[0] = (z * jax.nn.sigmoid(z)).astype(BF16)
        z = proj(P_ZB, ATT_WIDTH)
        zb_ref[0] = (z * jax.nn.sigmoid(z)).astype(BF16)
        for c in range(D_MODEL // 512):
            ga_ref[0, :, c * 512:(c + 1) * 512] = jax.nn.sigmoid(proj(P_GA + c * 512, 512)).astype(BF16)
            gb_ref[0, :, c * 512:(c + 1) * 512] = jax.nn.sigmoid(proj(P_GB + c * 512, 512)).astype(BF16)


def _proj_call(x, scale, shift, normw, w_packed, gvec, qn, kn, gmat, cos, sin, *, with_rest, tm):
    b, l, _ = x.shape
    nw = P_END if with_rest else P_STATE_END
    per_batch_mod = scale.shape[0] == b and b > 1
    mod_map = (lambda i, j: (i, 0, 0)) if per_batch_mod else (lambda i, j: (0, 0, 0))
    const2 = lambda i, j: (0, 0)
    row_map = lambda i, j: (i, j, 0)
    in_specs = [
        pl.BlockSpec((1, tm, D_MODEL), row_map),
        pl.BlockSpec((1, 1, D_MODEL), mod_map),
        pl.BlockSpec((1, 1, D_MODEL), mod_map),
        pl.BlockSpec((1, D_MODEL), const2),
        pl.BlockSpec((D_MODEL, nw), const2),
        pl.BlockSpec((8, LANES), const2),
        pl.BlockSpec((1, LANES), const2),
        pl.BlockSpec((1, LANES), const2),
        pl.BlockSpec((LANES, LANES), const2),
        pl.BlockSpec((tm, LANES), lambda i, j: (j, 0)),
        pl.BlockSpec((tm, LANES), lambda i, j: (j, 0)),
    ]
    widths = [(GDN_QKV, BF16), (LANES, F32), (2 * LANES, BF16)]
    if with_rest:
        widths += [(GDN_WIDTH, BF16), (ATT_WIDTH, BF16), (ATT_WIDTH, BF16), (D_MODEL, BF16), (D_MODEL, BF16)]
    out_specs = [pl.BlockSpec((1, tm, w), row_map) for w, _ in widths]
    out_shape = [jax.ShapeDtypeStruct((b, l, w), dt) for w, dt in widths]
    return pl.pallas_call(
        functools.partial(_proj_kernel, with_rest=with_rest),
        grid=(b, l // tm),
        in_specs=in_specs,
        out_specs=out_specs,
        out_shape=out_shape,
        compiler_params=_cparams(("arbitrary", "arbitrary")),
        name="proj_full" if with_rest else "proj_state",
    )(x, scale, shift, normw, w_packed[:, :nw], gvec, qn, kn, gmat, cos, sin)


HALO = 16


def _blockdiag(y, isf):
    zero = jnp.zeros_like(y)
    return jnp.concatenate([jnp.where(isf, y, zero), jnp.where(isf, zero, y)], axis=0)


def _short_conv_stage(raw_ref, prev_ref, next_ref, cw_ref, raw_scr, qkv_scr, tc):
    j = pl.program_id(1)
    n = pl.num_programs(1)
    raw_scr[0:HALO, :] = jnp.where(j > 0, prev_ref[0].astype(F32), 0.0)
    raw_scr[HALO:HALO + tc, :] = raw_ref[0].astype(F32)
    raw_scr[HALO + tc:2 * HALO + tc, :] = jnp.where(j < n - 1, next_ref[0].astype(F32), 0.0)
    half = SHORT_CONV // 2
    for cb in range(GDN_QKV // LANES):
        sl = slice(cb * LANES, (cb + 1) * LANES)
        y = raw_scr[HALO - half:HALO - half + tc, sl] * cw_ref[0:1, sl]
        for tap in range(1, SHORT_CONV):
            y = y + raw_scr[HALO - half + tap:HALO - half + tap + tc, sl] * cw_ref[tap:tap + 1, sl]
        y = y * jax.nn.sigmoid(y)
        if cb < 2 * GDN_HEADS:
            scale = GDN_DK ** -0.5 if cb < GDN_HEADS else 1.0
            y = y * (lax.rsqrt(jnp.sum(y * y, axis=-1, keepdims=True) + EPS) * scale)
        qkv_scr[:, sl] = y.astype(BF16)


def _prep_kernel(raw_ref, prev_ref, next_ref, cw_ref, gates_ref, u_ref, w_ref, kdt_ref, qd_ref, qk_ref, eg_ref,
                 raw_scr, qkv_ref, a_scr, t_scr, rhs_scr, *, tc):
    H = GDN_HEADS
    C = CHUNK
    _short_conv_stage(raw_ref, prev_ref, next_ref, cw_ref, raw_scr, qkv_ref, tc)
    G = gates_ref[0]
    gT = G.T[2 * H:4 * H, :]
    lane_t = lax.broadcasted_iota(jnp.int32, gT.shape, 1)
    pos = lane_t % C
    pre = gT
    suf = gT
    for s in (1, 2, 4, 8, 16, 32):
        pre = pre + jnp.where(pos >= s, pltpu.roll(pre, s, 1), 0.0)
        suf = suf + jnp.where(pos < C - s, pltpu.roll(suf, tc - s, 1), 0.0)
    row8 = lax.broadcasted_iota(jnp.int32, gT.shape, 0)
    gcT = jnp.where(row8 < H, pre, suf)
    gc = jnp.concatenate([gcT, jnp.zeros((LANES - 2 * H, tc), F32)], axis=0).T

    lane = lax.broadcasted_iota(jnp.int32, (C, LANES), 1)
    ii = lax.broadcasted_iota(jnp.int32, (C, LANES), 0)
    jj = lane % C
    isf = lane < C
    isb = jnp.logical_not(isf)
    incl = jnp.logical_or(jnp.logical_and(isf, ii >= jj), jnp.logical_and(isb, ii <= jj))
    strict = jnp.logical_or(jnp.logical_and(isf, ii > jj), jnp.logical_and(isb, ii < jj))
    eye2 = (ii == jj).astype(F32)
    same = {kk: (ii // kk) == (jj // kk) for kk in (2, 4, 8, 16, 32, 64)}
    lane8 = lax.broadcasted_iota(jnp.int32, (1, LANES), 1)

    def colb(arr, r0, c):
        return jnp.broadcast_to(arr[r0:r0 + C, c:c + 1], (C, LANES))

    for s in range(tc // C):
        r0 = s * C
        glrow = jnp.where(lane8 < H, gc[r0 + C - 1:r0 + C, :], gc[r0:r0 + 1, :])
        eg_ref[0, s] = jnp.broadcast_to(jnp.exp(glrow), (8, LANES))

    items = [(lt, 2 * p + hp, halfsel) for lt in range(tc // LANES) for p in range(H // 2)
             for halfsel in range(2) for hp in range(2)]
    gts = {}
    for lt in range(tc // LANES):
        gt = gcT[:, lt * LANES:(lt + 1) * LANES]
        gts[lt] = (gt, pltpu.roll(gt, C, 1))

    kd_parts = {}
    for idx, (lt, h, halfsel) in enumerate(items):
        gt, gt_r = gts[lt]
        r0 = (2 * lt + halfsel) * C
        if halfsel == 0:
            row_f, row_b = gt[h:h + 1, :], gt_r[H + h:H + h + 1, :]
        else:
            row_f, row_b = gt_r[h:h + 1, :], gt[H + h:H + h + 1, :]
        gc_row2 = jnp.broadcast_to(jnp.where(lane8 < C, row_f, row_b), (C, LANES))
        gcf = colb(gc, r0, h)
        gcb = colb(gc, r0, H + h)
        dec = jnp.where(incl, jnp.exp(jnp.where(incl, jnp.where(isf, gcf, gcb) - gc_row2, 0.0)), 0.0)
        bf = colb(G, r0, h)
        bb = colb(G, r0, H + h)
        qt = qkv_ref[r0:r0 + C, h * LANES:(h + 1) * LANES]
        kt = qkv_ref[r0:r0 + C, (H + h) * LANES:(H + h + 1) * LANES]
        vt = qkv_ref[r0:r0 + C, (2 * H + h) * LANES:(2 * H + h + 1) * LANES]
        kq = lax.dot_general(jnp.concatenate([kt, qt], axis=0), jnp.concatenate([kt, kt], axis=0),
                             (((1,), (1,)), ((), ())), preferred_element_type=F32)
        a2 = jnp.where(strict, jnp.where(isf, bf, bb) * kq[0:C] * dec, 0.0)
        a_scr[idx] = a2
        t_scr[idx] = eye2 - jnp.where(same[2], a2, 0.0)
        qk2 = kq[C:2 * C] * dec
        if h % 2 == 0:
            qk_even = qk2
        else:
            pc = (h // 2) * LANES
            qk_ref[0, r0:r0 + C, pc:pc + LANES] = jnp.where(isf, qk_even, pltpu.roll(qk2, C, 1)).astype(BF16)
            qk_ref[0, r0:r0 + C, H * C + pc:H * C + pc + LANES] = jnp.where(
                isf, pltpu.roll(qk_even, C, 1), qk2).astype(BF16)

        kf = kt.astype(F32)
        vf = vt.astype(F32)
        qf = qt.astype(F32)
        egf = jnp.exp(gcf)
        egb = jnp.exp(gcb)
        z2 = jnp.zeros((C, 2 * LANES), F32)
        rf = jnp.concatenate([vf * bf, kf * (bf * egf), z2], axis=1)
        rb = jnp.concatenate([z2, vf * bb, kf * (bb * egb)], axis=1)
        rhs_scr[idx] = jnp.concatenate([rf, rb], axis=0).astype(BF16)
        cf = h * LANES
        cb = (H + h) * LANES
        qd_ref[0, r0:r0 + C, cf:cf + LANES] = (qf * egf).astype(BF16)
        qd_ref[0, r0:r0 + C, cb:cb + LANES] = (qf * egb).astype(BF16)
        glf = jnp.broadcast_to(gc[r0 + C - 1:r0 + C, h:h + 1], (C, LANES))
        glb = jnp.broadcast_to(gc[r0:r0 + 1, H + h:H + h + 1], (C, LANES))
        kd_parts[(lt, h, halfsel)] = (kf * jnp.exp(glf - gcf), kf * jnp.exp(glb - gcb))
        if halfsel == 1:
            for d in range(2):
                kdt = jnp.concatenate([kd_parts[(lt, h, 0)][d], kd_parts[(lt, h, 1)][d]], axis=0).T
                kdt_ref[0, (d * H + h) * LANES:(d * H + h + 1) * LANES, lt * LANES:(lt + 1) * LANES] = kdt.astype(BF16)

    for kk in (2, 4, 8, 16, 32):
        emask = jnp.logical_and(same[2 * kk], jnp.logical_not(same[kk]))
        ps = []
        for idx in range(len(items)):
            e = jnp.where(emask, a_scr[idx], 0.0).astype(BF16)
            ps.append(jnp.dot(e, _blockdiag(t_scr[idx].astype(BF16), isf), preferred_element_type=F32))
        for idx in range(len(items)):
            t2 = t_scr[idx]
            t_scr[idx] = t2 - jnp.dot(t2.astype(BF16), _blockdiag(ps[idx].astype(BF16), isf),
                                      preferred_element_type=F32)

    for idx, (lt, h, halfsel) in enumerate(items):
        r0 = (2 * lt + halfsel) * C
        uw = jnp.dot(t_scr[idx].astype(BF16), rhs_scr[idx], preferred_element_type=F32)
        cf = h * LANES
        cb = (H + h) * LANES
        u_ref[0, r0:r0 + C, cf:cf + LANES] = uw[:, 0:LANES].astype(BF16)
        w_ref[0, r0:r0 + C, cf:cf + LANES] = uw[:, LANES:2 * LANES].astype(BF16)
        u_ref[0, r0:r0 + C, cb:cb + LANES] = uw[:, 2 * LANES:3 * LANES].astype(BF16)
        w_ref[0, r0:r0 + C, cb:cb + LANES] = uw[:, 3 * LANES:4 * LANES].astype(BF16)


def _prep_call(qkv_raw, conv_w, gates, *, tc):
    b, l, _ = qkv_raw.shape
    row_map = lambda i, j: (i, j, 0)
    wide = 2 * GDN_HEADS * LANES
    n_items = (tc // CHUNK) * GDN_HEADS
    hpt = tc // HALO
    nhalo = l // HALO
    return pl.pallas_call(
        functools.partial(_prep_kernel, tc=tc),
        grid=(b, l // tc),
        in_specs=[
            pl.BlockSpec((1, tc, GDN_QKV), row_map),
            pl.BlockSpec((1, HALO, GDN_QKV), lambda i, j: (i, jnp.maximum(j * hpt - 1, 0), 0)),
            pl.BlockSpec((1, HALO, GDN_QKV), lambda i, j: (i, jnp.minimum((j + 1) * hpt, nhalo - 1), 0)),
            pl.BlockSpec((8, GDN_QKV), lambda i, j: (0, 0)),
            pl.BlockSpec((1, tc, LANES), row_map),
        ],
        out_specs=[
            pl.BlockSpec((1, tc, wide), row_map),
            pl.BlockSpec((1, tc, wide), row_map),
            pl.BlockSpec((1, wide, tc), lambda i, j: (i, 0, j)),
            pl.BlockSpec((1, tc, wide), row_map),
            pl.BlockSpec((1, tc, 2 * GDN_HEADS * CHUNK), row_map),
            pl.BlockSpec((1, tc // CHUNK, 8, LANES), lambda i, j: (i, j, 0, 0)),
        ],
        out_shape=[
            jax.ShapeDtypeStruct((b, l, wide), BF16),
            jax.ShapeDtypeStruct((b, l, wide), BF16),
            jax.ShapeDtypeStruct((b, wide, l), BF16),
            jax.ShapeDtypeStruct((b, l, wide), BF16),
            jax.ShapeDtypeStruct((b, l, 2 * GDN_HEADS * CHUNK), BF16),
            jax.ShapeDtypeStruct((b, l // CHUNK, 8, LANES), F32),
        ],
        scratch_shapes=[
            pltpu.VMEM((tc + 2 * HALO, GDN_QKV), F32),
            pltpu.VMEM((tc, GDN_QKV), BF16),
            pltpu.VMEM((n_items, CHUNK, LANES), F32),
            pltpu.VMEM((n_items, CHUNK, LANES), F32),
            pltpu.VMEM((n_items, 2 * CHUNK, 4 * LANES), BF16),
        ],
        compiler_params=_cparams(("arbitrary", "arbitrary")),
        name="gdn_prep",
    )(qkv_raw, qkv_raw, qkv_raw, conv_w, gates)


def _chain_kernel(uf_ref, wf_ref, kdtf_ref, qdf_ref, qkf_ref, egf_ref,
                  ub_ref, wb_ref, kdtb_ref, qdb_ref, qkb_ref, egb_ref, s0_ref,
                  of_ref, ob_ref, sfin_ref, s_scr, *, tc):
    H = GDN_HEADS
    C = CHUNK
    i = pl.program_id(1)
    n = pl.num_programs(1)

    @pl.when(i == 0)
    def _():
        s_scr[...] = s0_ref[0]

    nch = tc // C
    zc = jnp.zeros((C, LANES), BF16)
    refs = ((uf_ref, wf_ref, kdtf_ref, qdf_ref, qkf_ref, egf_ref, of_ref),
            (ub_ref, wb_ref, kdtb_ref, qdb_ref, qkb_ref, egb_ref, ob_ref))
    chains = [(d, h) for d in range(2) for h in range(H)]
    for c in range(nch):
        rs = []
        for d, h in chains:
            u_ref, w_ref, kdt_ref, qd_ref, qk_ref, eg_ref, o_ref = refs[d]
            r0 = (c if d == 0 else nch - 1 - c) * C
            cs = h * LANES
            lhs = jnp.concatenate([w_ref[0, r0:r0 + C, cs:cs + LANES], qd_ref[0, r0:r0 + C, cs:cs + LANES]], axis=0)
            rs.append(jnp.dot(lhs, s_scr[d, h].astype(BF16), preferred_element_type=F32))
        vbs = []
        for (d, h), r in zip(chains, rs):
            u_ref, w_ref, kdt_ref, qd_ref, qk_ref, eg_ref, o_ref = refs[d]
            r0 = (c if d == 0 else nch - 1 - c) * C
            cs = h * LANES
            vbs.append((u_ref[0, r0:r0 + C, cs:cs + LANES].astype(F32) - r[0:C]).astype(BF16))
        for (d, h), vb in zip(chains, vbs):
            u_ref, w_ref, kdt_ref, qd_ref, qk_ref, eg_ref, o_ref = refs[d]
            cc = c if d == 0 else nch - 1 - c
            lt, par = divmod(cc, 2)
            cs = h * LANES
            v_par = jnp.concatenate([vb, zc], axis=0) if par == 0 else jnp.concatenate([zc, vb], axis=0)
            kdt = kdt_ref[0, cs:cs + LANES, lt * LANES:(lt + 1) * LANES]
            eg = jnp.broadcast_to(eg_ref[0, cc, 0:1, d * H + h:d * H + h + 1], (LANES, LANES))
            s_scr[d, h] = s_scr[d, h] * eg + jnp.dot(kdt, v_par, preferred_element_type=F32)
        for d in range(2):
            u_ref, w_ref, kdt_ref, qd_ref, qk_ref, eg_ref, o_ref = refs[d]
            r0 = (c if d == 0 else nch - 1 - c) * C
            for p in range(H // 2):
                v0, v1 = vbs[d * H + 2 * p], vbs[d * H + 2 * p + 1]
                v_pair = jnp.concatenate([jnp.concatenate([v0, zc], axis=1), jnp.concatenate([zc, v1], axis=1)], axis=0)
                intra = jnp.dot(qk_ref[0, r0:r0 + C, p * LANES:(p + 1) * LANES], v_pair, preferred_element_type=F32)
                for hp in range(2):
                    h = 2 * p + hp
                    o = rs[d * H + h][C:2 * C] + intra[:, hp * LANES:(hp + 1) * LANES]
                    o_ref[0, r0:r0 + C, h * LANES:(h + 1) * LANES] = o.astype(BF16)

    @pl.when(i == n - 1)
    def _():
        sfin_ref[0] = s_scr[...]


def _chain_call(u, w, kdt, qd, qk, eg, s0, *, tc):
    b, l, _ = u.shape
    n = l // tc
    hw = GDN_HEADS * LANES
    fwd = lambda i, j: (i, j, 0)
    bwd = lambda i, j: (i, n - 1 - j, 1)
    bwd0 = lambda i, j: (i, n - 1 - j, 0)
    qkw = GDN_HEADS * CHUNK
    nchunk = tc // CHUNK
    in_specs = [
        pl.BlockSpec((1, tc, hw), fwd), pl.BlockSpec((1, tc, hw), fwd),
        pl.BlockSpec((1, hw, tc), lambda i, j: (i, 0, j)),
        pl.BlockSpec((1, tc, hw), fwd), pl.BlockSpec((1, tc, qkw), fwd),
        pl.BlockSpec((1, nchunk, 8, LANES), lambda i, j: (i, j, 0, 0)),
        pl.BlockSpec((1, tc, hw), bwd), pl.BlockSpec((1, tc, hw), bwd),
        pl.BlockSpec((1, hw, tc), lambda i, j: (i, 1, n - 1 - j)),
        pl.BlockSpec((1, tc, hw), bwd), pl.BlockSpec((1, tc, qkw), bwd),
        pl.BlockSpec((1, nchunk, 8, LANES), lambda i, j: (i, n - 1 - j, 0, 0)),
        pl.BlockSpec((1, 2, GDN_HEADS, LANES, LANES), lambda i, j: (i, 0, 0, 0, 0)),
    ]
    out_specs = [
        pl.BlockSpec((1, tc, hw), fwd),
        pl.BlockSpec((1, tc, hw), bwd0),
        pl.BlockSpec((1, 2, GDN_HEADS, LANES, LANES), lambda i, j: (i, 0, 0, 0, 0)),
    ]
    out_shape = [
        jax.ShapeDtypeStruct((b, l, hw), BF16),
        jax.ShapeDtypeStruct((b, l, hw), BF16),
        jax.ShapeDtypeStruct((b, 2, GDN_HEADS, LANES, LANES), F32),
    ]
    return pl.pallas_call(
        functools.partial(_chain_kernel, tc=tc),
        grid=(b, n),
        in_specs=in_specs,
        out_specs=out_specs,
        out_shape=out_shape,
        scratch_shapes=[pltpu.VMEM((2, GDN_HEADS, LANES, LANES), F32)],
        compiler_params=_cparams(("arbitrary", "arbitrary")),
        name="gdn_chain",
    )(u, w, kdt, qd, qk, eg, u, w, kdt, qd, qk, eg, s0)


def _attn_kernel(sink_ref, q_ref, kvp_ref, kvo_ref, kvn_ref, kvc_ref, zb_ref, o_ref, *, local):
    nb = pl.num_programs(1)
    n = pl.program_id(1)
    T = ATT_BLOCK
    q = q_ref[0]
    qs = jnp.concatenate([q[:, g * LANES:(g + 1) * LANES] for g in range(ATT_GROUP)], axis=0)
    lane = lax.broadcasted_iota(jnp.int32, (1, LANES), 1)
    rowi = lax.broadcasted_iota(jnp.int32, (ATT_GROUP * T, T), 0) % T
    colj = lax.broadcasted_iota(jnp.int32, (ATT_GROUP * T, T), 1)
    grp = lax.broadcasted_iota(jnp.int32, (ATT_GROUP * T, 1), 0) // T

    pieces = [(kvc_ref, None)]
    if local:
        pieces += [(kvp_ref, jnp.logical_and(colj >= rowi, n > 0)),
                   (kvo_ref, None),
                   (kvn_ref, jnp.logical_and(colj <= rowi, n < nb - 1))]

    acc = jnp.zeros((ATT_GROUP * T, LANES), F32)
    for kvh in range(ATT_KV_HEADS):
        kmask = ((lane // ROPE_F) % ATT_KV_HEADS == kvh)
        vmask = (lane // ATT_HD == kvh)
        sink = jnp.zeros((ATT_GROUP * T, 1), F32)
        for g in range(ATT_GROUP):
            sink = jnp.where(grp == g, sink_ref[kvh * ATT_GROUP + g], sink)
        scores = []
        m = sink
        for ref, valid in pieces:
            kv = ref[0]
            kh = jnp.where(kmask, kv[:, 0:LANES], jnp.zeros_like(kv[:, 0:LANES]))
            s = lax.dot_general(qs, kh, (((1,), (1,)), ((), ())), preferred_element_type=F32)
            if valid is not None:
                s = jnp.where(valid, s, NEG_INF)
            scores.append(s)
            m = jnp.maximum(m, jnp.max(s, axis=-1, keepdims=True))
        denom = jnp.exp(sink - m)
        o_h = jnp.zeros((ATT_GROUP * T, LANES), F32)
        for (ref, _), s in zip(pieces, scores):
            kv = ref[0]
            vh = jnp.where(vmask, kv[:, LANES:2 * LANES], jnp.zeros_like(kv[:, 0:LANES]))
            p = jnp.exp(s - m)
            denom = denom + jnp.sum(p, axis=-1, keepdims=True)
            o_h = o_h + jnp.dot(p.astype(BF16), vh, preferred_element_type=F32)
        acc = acc + o_h / denom
    for g in range(ATT_GROUP):
        og = acc[g * T:(g + 1) * T, :] * zb_ref[0, :, g * LANES:(g + 1) * LANES].astype(F32)
        o_ref[0, :, g * LANES:(g + 1) * LANES] = og.astype(BF16)


def _attn_call(sink, q, kv, kvc, zb, *, local):
    b, l, _ = q.shape
    nb = l // ATT_BLOCK
    lc = kvc.shape[1]
    blk = lambda f: pl.BlockSpec((1, ATT_BLOCK, 2 * LANES), f)
    return pl.pallas_call(
        functools.partial(_attn_kernel, local=local),
        grid=(b, nb),
        in_specs=[
            pl.BlockSpec(memory_space=pltpu.SMEM),
            pl.BlockSpec((1, ATT_BLOCK, ATT_WIDTH), lambda i, j: (i, j, 0)),
            blk(lambda i, j: (i, jnp.maximum(j - 1, 0), 0)),
            blk(lambda i, j: (i, j, 0)),
            blk(lambda i, j: (i, jnp.minimum(j + 1, nb - 1), 0)),
            pl.BlockSpec((1, lc, 2 * LANES), lambda i, j: (i, 0, 0)),
            pl.BlockSpec((1, ATT_BLOCK, ATT_WIDTH), lambda i, j: (i, j, 0)),
        ],
        out_specs=pl.BlockSpec((1, ATT_BLOCK, ATT_WIDTH), lambda i, j: (i, j, 0)),
        out_shape=jax.ShapeDtypeStruct((b, l, ATT_WIDTH), BF16),
        compiler_params=_cparams(("arbitrary", "arbitrary")),
        name="attn_local" if local else "attn_ctx",
    )(sink, q, kv, kv, kv, kvc, zb)


def _merge_kernel(x_ref, gate_ref, of_ref, ob_ref, za_ref, yb_ref, ga_ref, gb_ref, gnw_ref,
                  wpa_ref, wpb_ref, wout_ref, o_ref):
    gnw = gnw_ref[...]
    ya_parts = []
    for h in range(GDN_HEADS):
        sl = slice(h * LANES, (h + 1) * LANES)
        o = of_ref[0, :, sl].astype(F32) + ob_ref[0, :, sl].astype(F32)
        on = o * lax.rsqrt(jnp.mean(o * o, axis=-1, keepdims=True) + EPS) * gnw
        ya_parts.append((on * za_ref[0, :, sl].astype(F32)).astype(BF16))
    ya = jnp.concatenate(ya_parts, axis=1)
    pa = jnp.dot(ya, wpa_ref[...], preferred_element_type=F32)
    pb = jnp.dot(yb_ref[0], wpb_ref[...], preferred_element_type=F32)
    y = ga_ref[0].astype(F32) * pa + gb_ref[0].astype(F32) * pb
    out = jnp.dot(y.astype(BF16), wout_ref[...], preferred_element_type=F32)
    o_ref[0] = x_ref[0] + gate_ref[0] * out


def _merge_call(x, gate, of, ob, za, yb, ga, gb, gnw, wpa, wpb, wout, *, tm):
    b, l, _ = x.shape
    per_batch_mod = gate.shape[0] == b and b > 1
    mod_map = (lambda i, j: (i, 0, 0)) if per_batch_mod else (lambda i, j: (0, 0, 0))
    row = lambda w: pl.BlockSpec((1, tm, w), lambda i, j: (i, j, 0))
    const2 = lambda i, j: (0, 0)
    return pl.pallas_call(
        _merge_kernel,
        grid=(b, l // tm),
        in_specs=[
            row(D_MODEL), pl.BlockSpec((1, 1, D_MODEL), mod_map),
            row(GDN_WIDTH), row(GDN_WIDTH), row(GDN_WIDTH), row(ATT_WIDTH), row(D_MODEL), row(D_MODEL),
            pl.BlockSpec((1, LANES), const2),
            pl.BlockSpec((GDN_WIDTH, D_MODEL), const2),
            pl.BlockSpec((ATT_WIDTH, D_MODEL), const2),
            pl.BlockSpec((D_MODEL, D_MODEL), const2),
        ],
        out_specs=row(D_MODEL),
        out_shape=jax.ShapeDtypeStruct((b, l, D_MODEL), F32),
        compiler_params=_cparams(("arbitrary", "arbitrary")),
        name="merge",
    )(x, gate, of, ob, za, yb, ga, gb, gnw, wpa, wpb, wout)


def _pick_tile(l, pref):
    t = min(pref, l)
    while l % t:
        t //= 2
    return t


def kernel(x, c, ctx, c_ctx, norm_w, w_mod, b_mod, w_in, conv_w, a_log, dt_bias, gdn_norm_w,
           q_norm_w, k_norm_w, sink, w_proj_a, w_proj_b, w_out):
    b, l, _ = x.shape
    lc = ctx.shape[1]
    assert l % ATT_BLOCK == 0 and lc % ATT_BLOCK == 0 and x.shape[2] == D_MODEL

    rows = ((b + 1 + 7) // 8) * 8
    c_all = jnp.concatenate([c, c_ctx[None, :], jnp.zeros((rows - b - 1, D_MODEL), F32)], axis=0)
    mod = _mod_call(c_all, w_mod, b_mod)

    cos_l, sin_l = _rope_tables(l)
    cos_c = jnp.ones((lc, LANES), F32)
    sin_c = jnp.zeros((lc, LANES), F32)
    grp = (np.arange(LANES) // ROPE_F) % ATT_KV_HEADS
    gmat = jnp.asarray((grp[:, None] == grp[None, :]).astype(np.float32), dtype=BF16)
    s0 = jnp.zeros((b, 2, GDN_HEADS, LANES, LANES), F32)
    tm_l = _pick_tile(l, 512)
    tm_c = _pick_tile(lc, 256)
    tc_l = _pick_tile(l, 256)
    tc_c = _pick_tile(lc, 256)

    w_packed_all = _pack_w_in(w_in)
    ng = 2 * GDN_HEADS
    gpad = jnp.zeros((DEPTH, ng), F32)
    gvec_all = jnp.concatenate([
        jnp.concatenate([gpad, jnp.exp(a_log.astype(F32)).reshape(DEPTH, ng),
                         jnp.zeros((DEPTH, LANES - 2 * ng), F32)], axis=1)[:, None, :],
        jnp.concatenate([gpad, dt_bias.astype(F32).reshape(DEPTH, ng),
                         jnp.zeros((DEPTH, LANES - 2 * ng), F32)], axis=1)[:, None, :],
        jnp.zeros((DEPTH, 6, LANES), F32)], axis=1)
    qn_all = jax.vmap(_norm_lanes)(q_norm_w) * (ATT_HD ** -0.5)
    kn_all = jax.vmap(_norm_lanes)(k_norm_w)
    cw_all = jnp.pad(conv_w, ((0, 0), (0, 8 - SHORT_CONV), (0, 0)))
    wpa_all = w_proj_a.astype(BF16)
    wpb_all = jnp.swapaxes(w_proj_b.reshape(DEPTH, ATT_KV_HEADS, ATT_GROUP, ATT_HD, D_MODEL), 1, 2).reshape(
        DEPTH, ATT_WIDTH, D_MODEL).astype(BF16)
    wout_all = w_out.astype(BF16)

    for i in range(DEPTH):
        update_ctx = i < DEPTH - 1
        shift, scale, gate = (mod[i, :, k * D_MODEL:(k + 1) * D_MODEL] for k in range(3))
        shift_l, scale_l, gate_l = (t[:b, None, :] for t in (shift, scale, gate))
        shift_c, scale_c, gate_c = (t[b:b + 1, None, :] for t in (shift, scale, gate))
        w_packed = w_packed_all[i]
        normw = norm_w[i][None, :]
        gvec = gvec_all[i]
        qn = qn_all[i][None, :]
        kn = kn_all[i][None, :]
        cw = cw_all[i]
        gnw = gdn_norm_w[i][None, :]
        wpa = wpa_all[i]
        wpb = wpb_all[i]
        wout = wout_all[i]

        outs_c = _proj_call(ctx, scale_c, shift_c, normw, w_packed, gvec, qn, kn, gmat, cos_c, sin_c,
                            with_rest=update_ctx, tm=tm_c)
        qkv_c, gates_c, kv_c = outs_c[:3]
        prep_c = _prep_call(qkv_c, cw, gates_c, tc=tc_c)
        of_c, ob_c, s_ctx = _chain_call(*prep_c, s0, tc=tc_c)

        qkv_l, gates_l, kv_l, za_l, q_l, zb_l, ga_l, gb_l = _proj_call(
            x, scale_l, shift_l, normw, w_packed, gvec, qn, kn, gmat, cos_l, sin_l, with_rest=True, tm=tm_l)
        prep_l = _prep_call(qkv_l, cw, gates_l, tc=tc_l)
        of_l, ob_l, _ = _chain_call(*prep_l, s_ctx, tc=tc_l)
        yb_l = _attn_call(sink[i], q_l, kv_l, kv_c, zb_l, local=True)
        x_new = _merge_call(x, gate_l, of_l, ob_l, za_l, yb_l, ga_l, gb_l, gnw, wpa, wpb, wout, tm=tm_l)

        if update_ctx:
            za_c, q_c, zb_c, ga_c, gb_c = outs_c[3:]
            yb_c = _attn_call(sink[i], q_c, kv_c, kv_c, zb_c, local=False)
            ctx = _merge_call(ctx, gate_c, of_c, ob_c, za_c, yb_c, ga_c, gb_c, gnw, wpa, wpb, wout, tm=tm_c)
        x = x_new
    return x
```

```python
import functools
import math

import numpy as np
import jax
import jax.numpy as jnp
from jax import lax
from jax.experimental import pallas as pl
from jax.experimental.pallas import tpu as pltpu

F32 = jnp.float32
BF16 = jnp.bfloat16

D_MODEL = 1024
DEPTH = 2
GRID_W = 64
EPS = 1e-6
NEG_INF = -1e30

GDN_HEADS = 4
GDN_DK = 128
GDN_DV = 128
GDN_QKV = GDN_HEADS * (2 * GDN_DK + GDN_DV)
GDN_WIDTH = GDN_HEADS * GDN_DV
SHORT_CONV = 5
CHUNK = 64

ATT_HEADS = 8
ATT_KV_HEADS = 2
ATT_GROUP = ATT_HEADS // ATT_KV_HEADS
ATT_HD = 64
ATT_WIDTH = ATT_HEADS * ATT_HD
ATT_KV_WIDTH = ATT_KV_HEADS * ATT_HD
ATT_BLOCK = 128
ROPE_BASE = 10000.0
AXIS_DIM = ATT_HD // 2
ROPE_F = AXIS_DIM // 2

N_STATE = GDN_QKV + 4 * GDN_HEADS + 2 * ATT_KV_WIDTH

LANES = 128
P_QKV = 0
P_GATE = GDN_QKV
P_K = P_GATE + LANES
P_V = P_K + ATT_KV_WIDTH
P_STATE_END = P_V + ATT_KV_WIDTH
P_ZA = P_STATE_END
P_Q = P_ZA + GDN_WIDTH
P_ZB = P_Q + ATT_WIDTH
P_GA = P_ZB + ATT_WIDTH
P_GB = P_GA + D_MODEL
P_END = P_GB + D_MODEL

VMEM_LIMIT = 56 * 1024 * 1024


def _cparams(sem):
    return pltpu.CompilerParams(dimension_semantics=sem, vmem_limit_bytes=VMEM_LIMIT)


def _k_cols(w):
    lead = w.shape[:-1]
    w = w.reshape(lead + (ATT_KV_HEADS, 4, ROPE_F))
    return jnp.swapaxes(w, -3, -2).reshape(lead + (ATT_KV_WIDTH,))


def _q_cols(w):
    lead = w.shape[:-1]
    w = w.reshape(lead + (ATT_KV_HEADS, ATT_GROUP, 4, ROPE_F))
    return jnp.moveaxis(w, -4, -2).reshape(lead + (ATT_WIDTH,))


def _o_cols(w):
    lead = w.shape[:-1]
    w = w.reshape(lead + (ATT_KV_HEADS, ATT_GROUP, ATT_HD))
    return jnp.swapaxes(w, -3, -2).reshape(lead + (ATT_WIDTH,))


def _norm_lanes(w):
    w = w.reshape(4, 1, ROPE_F)
    return jnp.broadcast_to(w, (4, ATT_KV_HEADS, ROPE_F)).reshape(LANES)


def _pack_w_in(w_in):
    w_in = w_in.astype(BF16)
    o = 0
    qkv = w_in[..., o:o + GDN_QKV]; o += GDN_QKV
    gates = w_in[..., o:o + 4 * GDN_HEADS]; o += 4 * GDN_HEADS
    kb = w_in[..., o:o + ATT_KV_WIDTH]; o += ATT_KV_WIDTH
    vb = w_in[..., o:o + ATT_KV_WIDTH]; o += ATT_KV_WIDTH
    za = w_in[..., o:o + GDN_WIDTH]; o += GDN_WIDTH
    qb = w_in[..., o:o + ATT_WIDTH]; o += ATT_WIDTH
    zb = w_in[..., o:o + ATT_WIDTH]; o += ATT_WIDTH
    ga = w_in[..., o:o + D_MODEL]; o += D_MODEL
    gb = w_in[..., o:o + D_MODEL]
    gates = jnp.concatenate([gates, jnp.zeros(gates.shape[:-1] + (LANES - 4 * GDN_HEADS,), BF16)], axis=-1)
    return jnp.concatenate(
        [qkv, gates, _k_cols(kb), vb, za, _q_cols(qb), _o_cols(zb), ga, gb], axis=-1)


def _rope_tables(seq_len):
    t = jnp.arange(seq_len, dtype=F32)
    pos_row = jnp.floor(t / GRID_W)
    pos_col = t - pos_row * GRID_W
    inv = ROPE_BASE ** (-jnp.arange(0, AXIS_DIM, 2, dtype=F32) / AXIS_DIM)
    ang_r = pos_row[:, None] * inv[None, :]
    ang_c = pos_col[:, None] * inv[None, :]

    def lay(a):
        return jnp.concatenate([a, a], axis=1)

    cos = jnp.concatenate([lay(jnp.cos(ang_r)), lay(jnp.cos(ang_r)), lay(jnp.cos(ang_c)), lay(jnp.cos(ang_c))], axis=1)
    sin = jnp.concatenate([-lay(jnp.sin(ang_r)), lay(jnp.sin(ang_r)), -lay(jnp.sin(ang_c)), lay(jnp.sin(ang_c))], axis=1)
    return cos, sin


def _mod_kernel(c_ref, w_ref, b_ref, o_ref):
    c = c_ref[...]
    s = c * jax.nn.sigmoid(c)
    o_ref[0] = jnp.dot(s.astype(BF16), w_ref[0].astype(BF16), preferred_element_type=F32) + b_ref[0]


def _mod_call(c_all, w_mod, b_mod):
    rows = c_all.shape[0]
    nblk = 3
    return pl.pallas_call(
        _mod_kernel,
        grid=(DEPTH, nblk),
        in_specs=[
            pl.BlockSpec((rows, D_MODEL), lambda l, j: (0, 0)),
            pl.BlockSpec((1, D_MODEL, D_MODEL), lambda l, j: (l, 0, j)),
            pl.BlockSpec((1, 1, D_MODEL), lambda l, j: (l, 0, j)),
        ],
        out_specs=pl.BlockSpec((1, rows, D_MODEL), lambda l, j: (l, 0, j)),
        out_shape=jax.ShapeDtypeStruct((DEPTH, rows, 3 * D_MODEL), F32),
        compiler_params=_cparams(("arbitrary", "arbitrary")),
        name="mod",
    )(c_all, w_mod, b_mod.reshape(DEPTH, 1, 3 * D_MODEL))


def _swap32(t):
    lane = lax.broadcasted_iota(jnp.int32, t.shape, 1)
    even = (lane // 32) % 2 == 0
    return jnp.where(even, pltpu.roll(t, 96, 1), pltpu.roll(t, 32, 1))


def _headnorm_rope(parts, nws, gmat, cos, sin):
    t = jnp.concatenate(parts, axis=0) if len(parts) > 1 else parts[0]
    ss = jnp.dot((t * t).astype(BF16), gmat, preferred_element_type=F32)
    inv = lax.rsqrt(ss * (1.0 / ATT_HD) + EPS)
    tm = parts[0].shape[0]
    outs = []
    for i, nw in enumerate(nws):
        tn = parts[i] * inv[i * tm:(i + 1) * tm] * nw
        outs.append(tn * cos + _swap32(tn) * sin)
    return outs


def _softplus(x):
    return jnp.maximum(x, 0.0) + jnp.log1p(jnp.exp(-jnp.abs(x)))


def _proj_kernel(x_ref, scale_ref, shift_ref, normw_ref, w_ref, gvec_ref, qn_ref, kn_ref, gmat_ref,
                 cos_ref, sin_ref, *out_refs, with_rest):
    if with_rest:
        qkv_ref, gates_ref, kv_ref, za_ref, q_ref, zb_ref, ga_ref, gb_ref = out_refs
    else:
        qkv_ref, gates_ref, kv_ref = out_refs
    x = x_ref[0]
    ms = jnp.mean(x * x, axis=-1, keepdims=True)
    a = normw_ref[...] * (1.0 + scale_ref[0])
    h = (x * lax.rsqrt(ms + EPS) * a + shift_ref[0]).astype(BF16)

    def proj(lo, width):
        return jnp.dot(h, w_ref[:, lo:lo + width], preferred_element_type=F32)

    for c in range(GDN_QKV // 512):
        qkv_ref[0, :, c * 512:(c + 1) * 512] = proj(P_QKV + c * 512, 512).astype(BF16)

    g = proj(P_GATE, LANES)
    lane = lax.broadcasted_iota(jnp.int32, g.shape, 1)
    beta = jax.nn.sigmoid(g)
    dec = -gvec_ref[0:1, :] * _softplus(g + gvec_ref[1:2, :])
    gates_ref[0] = jnp.where(lane < 2 * GDN_HEADS, beta, dec)

    gmat = gmat_ref[...]
    cos = cos_ref[...]
    sin = sin_ref[...]
    kv_ref[0, :, LANES:2 * LANES] = proj(P_V, LANES).astype(BF16)
    parts = [proj(P_K, LANES)]
    nws = [kn_ref[...]]
    if with_rest:
        qraw = proj(P_Q, ATT_WIDTH)
        parts += [qraw[:, gi * LANES:(gi + 1) * LANES] for gi in range(ATT_GROUP)]
        nws += [qn_ref[...]] * ATT_GROUP
    normed = _headnorm_rope(parts, nws, gmat, cos, sin)
    kv_ref[0, :, 0:LANES] = normed[0].astype(BF16)

    if with_rest:
        for gi in range(ATT_GROUP):
            q_ref[0, :, gi * LANES:(gi + 1) * LANES] = normed[1 + gi].astype(BF16)
        z = proj(P_ZA, GDN_WIDTH)
        za_ref[0] = (z * jax.nn.sigmoid(z)).astype(BF16)
        z = proj(P_ZB, ATT_WIDTH)
        zb_ref[0] = (z * jax.nn.sigmoid(z)).astype(BF16)
        for c in range(D_MODEL // 512):
            ga_ref[0, :, c * 512:(c + 1) * 512] = jax.nn.sigmoid(proj(P_GA + c * 512, 512)).astype(BF16)
            gb_ref[0, :, c * 512:(c + 1) * 512] = jax.nn.sigmoid(proj(P_GB + c * 512, 512)).astype(BF16)


def _proj_call(x, scale, shift, normw, w_packed, gvec, qn, kn, gmat, cos, sin, *, with_rest, tm):
    b, l, _ = x.shape
    nw = P_END if with_rest else P_STATE_END
    per_batch_mod = scale.shape[0] == b and b > 1
    mod_map = (lambda i, j: (i, 0, 0)) if per_batch_mod else (lambda i, j: (0, 0, 0))
    const2 = lambda i, j: (0, 0)
    row_map = lambda i, j: (i, j, 0)
    in_specs = [
        pl.BlockSpec((1, tm, D_MODEL), row_map),
        pl.BlockSpec((1, 1, D_MODEL), mod_map),
        pl.BlockSpec((1, 1, D_MODEL), mod_map),
        pl.BlockSpec((1, D_MODEL), const2),
        pl.BlockSpec((D_MODEL, nw), const2),
        pl.BlockSpec((8, LANES), const2),
        pl.BlockSpec((1, LANES), const2),
        pl.BlockSpec((1, LANES), const2),
        pl.BlockSpec((LANES, LANES), const2),
        pl.BlockSpec((tm, LANES), lambda i, j: (j, 0)),
        pl.BlockSpec((tm, LANES), lambda i, j: (j, 0)),
    ]
    widths = [(GDN_QKV, BF16), (LANES, F32), (2 * LANES, BF16)]
    if with_rest:
        widths += [(GDN_WIDTH, BF16), (ATT_WIDTH, BF16), (ATT_WIDTH, BF16), (D_MODEL, BF16), (D_MODEL, BF16)]
    out_specs = [pl.BlockSpec((1, tm, w), row_map) for w, _ in widths]
    out_shape = [jax.ShapeDtypeStruct((b, l, w), dt) for w, dt in widths]
    return pl.pallas_call(
        functools.partial(_proj_kernel, with_rest=with_rest),
        grid=(b, l // tm),
        in_specs=in_specs,
        out_specs=out_specs,
        out_shape=out_shape,
        compiler_params=_cparams(("arbitrary", "arbitrary")),
        name="proj_full" if with_rest else "proj_state",
    )(x, scale, shift, normw, w_packed[:, :nw], gvec, qn, kn, gmat, cos, sin)


HALO = 16


def _blockdiag(y, isf):
    zero = jnp.zeros_like(y)
    return jnp.concatenate([jnp.where(isf, y, zero), jnp.where(isf, zero, y)], axis=0)


def _short_conv_stage(raw_ref, prev_ref, next_ref, cw_ref, raw_scr, qkv_scr, tc):
    j = pl.program_id(1)
    n = pl.num_programs(1)
    half = SHORT_CONV // 2
    for cb in range(GDN_QKV // LANES):
        sl = slice(cb * LANES, (cb + 1) * LANES)
        raw_scr[cb, 0:HALO, :] = jnp.where(j > 0, prev_ref[0, :, sl].astype(F32), 0.0)
        raw_scr[cb, HALO:HALO + tc, :] = raw_ref[0, :, sl].astype(F32)
        raw_scr[cb, HALO + tc:2 * HALO + tc, :] = jnp.where(j < n - 1, next_ref[0, :, sl].astype(F32), 0.0)
        y = raw_scr[cb, HALO - half:HALO - half + tc, :] * cw_ref[0:1, sl]
        for tap in range(1, SHORT_CONV):
            y = y + raw_scr[cb, HALO - half + tap:HALO - half + tap + tc, :] * cw_ref[tap:tap + 1, sl]
        y = y * jax.nn.sigmoid(y)
        if cb < 2 * GDN_HEADS:
            scale = GDN_DK ** -0.5 if cb < GDN_HEADS else 1.0
            y = y * (lax.rsqrt(jnp.sum(y * y, axis=-1, keepdims=True) + EPS) * scale)
        qkv_scr[:, sl] = y.astype(BF16)


def _prep_kernel(raw_ref, prev_ref, next_ref, cw_ref, gates_ref, u_ref, w_ref, kdt_ref, qd_ref, qk_ref, eg_ref,
                 raw_scr, qkv_ref, a_scr, t_scr, rhs_scr, *, tc):
    H = GDN_HEADS
    C = CHUNK

    @pl.when(jnp.logical_and(pl.program_id(0) == 0, pl.program_id(1) == 0))
    def _():
        rhs_scr[...] = jnp.zeros(rhs_scr.shape, BF16)

    _short_conv_stage(raw_ref, prev_ref, next_ref, cw_ref, raw_scr, qkv_ref, tc)
    G = gates_ref[0]
    gT = G.T[2 * H:4 * H, :]
    lane_t = lax.broadcasted_iota(jnp.int32, gT.shape, 1)
    pos = lane_t % C
    pre = gT
    suf = gT
    for s in (1, 2, 4, 8, 16, 32):
        pre = pre + jnp.where(pos >= s, pltpu.roll(pre, s, 1), 0.0)
        suf = suf + jnp.where(pos < C - s, pltpu.roll(suf, tc - s, 1), 0.0)
    row8 = lax.broadcasted_iota(jnp.int32, gT.shape, 0)
    gcT = jnp.where(row8 < H, pre, suf)
    gc = jnp.concatenate([gcT, jnp.zeros((LANES - 2 * H, tc), F32)], axis=0).T

    lane = lax.broadcasted_iota(jnp.int32, (C, LANES), 1)
    ii = lax.broadcasted_iota(jnp.int32, (C, LANES), 0)
    jj = lane % C
    isf = lane < C
    isb = jnp.logical_not(isf)
    incl = jnp.logical_or(jnp.logical_and(isf, ii >= jj), jnp.logical_and(isb, ii <= jj))
    strict = jnp.logical_or(jnp.logical_and(isf, ii > jj), jnp.logical_and(isb, ii < jj))
    eye2 = (ii == jj).astype(F32)
    same = {kk: (ii // kk) == (jj // kk) for kk in (2, 4, 8, 16, 32, 64)}
    lane8 = lax.broadcasted_iota(jnp.int32, (1, LANES), 1)

    def colb(arr, r0, c):
        return jnp.broadcast_to(arr[r0:r0 + C, c:c + 1], (C, LANES))

    for s in range(tc // C):
        r0 = s * C
        glrow = jnp.where(lane8 < H, gc[r0 + C - 1:r0 + C, :], gc[r0:r0 + 1, :])
        eg_ref[0, s] = jnp.broadcast_to(jnp.exp(glrow), (8, LANES))

    items = [(lt, 2 * p + hp, halfsel) for lt in range(tc // LANES) for p in range(H // 2)
             for halfsel in range(2) for hp in range(2)]
    gts = {}
    for lt in range(tc // LANES):
        gt = gcT[:, lt * LANES:(lt + 1) * LANES]
        gts[lt] = (gt, pltpu.roll(gt, C, 1))

    kd_parts = {}
    qk_hold = {}

    def stage1(idx):
        lt, h, halfsel = items[idx]
        gt, gt_r = gts[lt]
        r0 = (2 * lt + halfsel) * C
        if halfsel == 0:
            row_f, row_b = gt[h:h + 1, :], gt_r[H + h:H + h + 1, :]
        else:
            row_f, row_b = gt_r[h:h + 1, :], gt[H + h:H + h + 1, :]
        gc_row2 = jnp.broadcast_to(jnp.where(lane8 < C, row_f, row_b), (C, LANES))
        gcf = colb(gc, r0, h)
        gcb = colb(gc, r0, H + h)
        dec = jnp.where(incl, jnp.exp(jnp.where(incl, jnp.where(isf, gcf, gcb) - gc_row2, 0.0)), 0.0)
        bf = colb(G, r0, h)
        bb = colb(G, r0, H + h)
        qt = qkv_ref[r0:r0 + C, h * LANES:(h + 1) * LANES]
        kt = qkv_ref[r0:r0 + C, (H + h) * LANES:(H + h + 1) * LANES]
        vt = qkv_ref[r0:r0 + C, (2 * H + h) * LANES:(2 * H + h + 1) * LANES]
        kq = lax.dot_general(jnp.concatenate([kt, qt], axis=0), jnp.concatenate([kt, kt], axis=0),
                             (((1,), (1,)), ((), ())), preferred_element_type=F32)
        a2 = jnp.where(strict, jnp.where(isf, bf, bb) * kq[0:C] * dec, 0.0)
        a_scr[idx] = a2
        t_scr[idx] = eye2 - jnp.where(same[2], a2, 0.0)
        qk2 = kq[C:2 * C] * dec
        if h % 2 == 0:
            qk_hold[(lt, h, halfsel)] = qk2
        else:
            qk_even = qk_hold.pop((lt, h - 1, halfsel))
            pc = (h // 2) * LANES
            qk_ref[0, r0:r0 + C, pc:pc + LANES] = jnp.where(isf, qk_even, pltpu.roll(qk2, C, 1)).astype(BF16)
            qk_ref[0, r0:r0 + C, H * C + pc:H * C + pc + LANES] = jnp.where(
                isf, pltpu.roll(qk_even, C, 1), qk2).astype(BF16)

        kf = kt.astype(F32)
        vf = vt.astype(F32)
        qf = qt.astype(F32)
        egf = jnp.exp(gcf)
        egb = jnp.exp(gcb)
        rhs_scr[idx, 0:C, 0:2 * LANES] = jnp.concatenate([vf * bf, kf * (bf * egf)], axis=1).astype(BF16)
        rhs_scr[idx, C:2 * C, 2 * LANES:4 * LANES] = jnp.concatenate([vf * bb, kf * (bb * egb)], axis=1).astype(BF16)
        cf = h * LANES
        cb = (H + h) * LANES
        qd_ref[0, r0:r0 + C, cf:cf + LANES] = (qf * egf).astype(BF16)
        qd_ref[0, r0:r0 + C, cb:cb + LANES] = (qf * egb).astype(BF16)
        glf = jnp.broadcast_to(gc[r0 + C - 1:r0 + C, h:h + 1], (C, LANES))
        glb = jnp.broadcast_to(gc[r0:r0 + 1, H + h:H + h + 1], (C, LANES))
        kd_parts[(lt, h, halfsel)] = (kf * jnp.exp(glf - gcf), kf * jnp.exp(glb - gcb))
        if halfsel == 1:
            for d in range(2):
                kdt = jnp.concatenate([kd_parts[(lt, h, 0)][d], kd_parts[(lt, h, 1)][d]], axis=0).T
                kdt_ref[0, (d * H + h) * LANES:(d * H + h + 1) * LANES, lt * LANES:(lt + 1) * LANES] = kdt.astype(BF16)

    levels = (2, 4, 8, 16, 32)

    def round_a(kk, idxs):
        emask = jnp.logical_and(same[2 * kk], jnp.logical_not(same[kk]))
        ps = []
        for idx in idxs:
            e = jnp.where(emask, a_scr[idx], 0.0).astype(BF16)
            ps.append(jnp.dot(e, _blockdiag(t_scr[idx].astype(BF16), isf), preferred_element_type=F32))
        return ps

    def round_b(idxs, ps):
        for idx, p in zip(idxs, ps):
            t2 = t_scr[idx]
            t_scr[idx] = t2 - jnp.dot(t2.astype(BF16), _blockdiag(p.astype(BF16), isf),
                                      preferred_element_type=F32)

    def stage3(idx):
        lt, h, halfsel = items[idx]
        r0 = (2 * lt + halfsel) * C
        uw = jnp.dot(t_scr[idx].astype(BF16), rhs_scr[idx], preferred_element_type=F32)
        cf = h * LANES
        cb = (H + h) * LANES
        u_ref[0, r0:r0 + C, cf:cf + LANES] = uw[:, 0:LANES].astype(BF16)
        w_ref[0, r0:r0 + C, cf:cf + LANES] = uw[:, LANES:2 * LANES].astype(BF16)
        u_ref[0, r0:r0 + C, cb:cb + LANES] = uw[:, 2 * LANES:3 * LANES].astype(BF16)
        w_ref[0, r0:r0 + C, cb:cb + LANES] = uw[:, 3 * LANES:4 * LANES].astype(BF16)

    everything = list(range(len(items)))
    for idx in everything:
        stage1(idx)
    for kk in levels:
        round_b(everything, round_a(kk, everything))
    for idx in everything:
        stage3(idx)


def _prep_call(qkv_raw, conv_w, gates, *, tc):
    b, l, _ = qkv_raw.shape
    row_map = lambda i, j: (i, j, 0)
    wide = 2 * GDN_HEADS * LANES
    n_items = (tc // CHUNK) * GDN_HEADS
    hpt = tc // HALO
    nhalo = l // HALO
    n = l // tc
    return pl.pallas_call(
        functools.partial(_prep_kernel, tc=tc),
        grid=(b, l // tc),
        in_specs=[
            pl.BlockSpec((1, tc, GDN_QKV), row_map),
            pl.BlockSpec((1, HALO, GDN_QKV), lambda i, j: (i, jnp.maximum(j * hpt - 1, 0), 0)),
            pl.BlockSpec((1, HALO, GDN_QKV), lambda i, j: (i, jnp.minimum((j + 1) * hpt, nhalo - 1), 0)),
            pl.BlockSpec((8, GDN_QKV), lambda i, j: (0, 0)),
            pl.BlockSpec((1, tc, LANES), row_map),
        ],
        out_specs=[
            pl.BlockSpec((1, tc, wide), row_map),
            pl.BlockSpec((1, tc, wide), row_map),
            pl.BlockSpec((1, wide, tc), lambda i, j: (i, 0, j)),
            pl.BlockSpec((1, tc, wide), row_map),
            pl.BlockSpec((1, tc, 2 * GDN_HEADS * CHUNK), row_map),
            pl.BlockSpec((1, tc // CHUNK, 8, LANES), lambda i, j: (i, j, 0, 0)),
        ],
        out_shape=[
            jax.ShapeDtypeStruct((b, l, wide), BF16),
            jax.ShapeDtypeStruct((b, l, wide), BF16),
            jax.ShapeDtypeStruct((b, wide, l), BF16),
            jax.ShapeDtypeStruct((b, l, wide), BF16),
            jax.ShapeDtypeStruct((b, l, 2 * GDN_HEADS * CHUNK), BF16),
            jax.ShapeDtypeStruct((b, l // CHUNK, 8, LANES), F32),
        ],
        scratch_shapes=[
            pltpu.VMEM((GDN_QKV // LANES, tc + 2 * HALO, LANES), F32),
            pltpu.VMEM((tc, GDN_QKV), BF16),
            pltpu.VMEM((n_items, CHUNK, LANES), F32),
            pltpu.VMEM((n_items, CHUNK, LANES), F32),
            pltpu.VMEM((n_items, 2 * CHUNK, 4 * LANES), BF16),
        ],
        compiler_params=_cparams(("arbitrary", "arbitrary")),
        name="gdn_prep",
    )(qkv_raw, qkv_raw, qkv_raw, conv_w, gates)


def _chain_kernel(uf_ref, wf_ref, kdtf_ref, qdf_ref, qkf_ref, egf_ref,
                  ub_ref, wb_ref, kdtb_ref, qdb_ref, qkb_ref, egb_ref, s0_ref,
                  of_ref, ob_ref, sfin_ref, s_scr, *, tc):
    H = GDN_HEADS
    C = CHUNK
    i = pl.program_id(1)
    n = pl.num_programs(1)

    @pl.when(i == 0)
    def _():
        s_scr[...] = s0_ref[0]

    nch = tc // C
    zc = jnp.zeros((C, LANES), BF16)
    refs = ((uf_ref, wf_ref, kdtf_ref, qdf_ref, qkf_ref, egf_ref, of_ref),
            (ub_ref, wb_ref, kdtb_ref, qdb_ref, qkb_ref, egb_ref, ob_ref))
    chains = [(d, h) for d in range(2) for h in range(H)]
    for c in range(nch):
        rs = []
        for d, h in chains:
            u_ref, w_ref, kdt_ref, qd_ref, qk_ref, eg_ref, o_ref = refs[d]
            r0 = (c if d == 0 else nch - 1 - c) * C
            cs = h * LANES
            lhs = jnp.concatenate([w_ref[0, r0:r0 + C, cs:cs + LANES], qd_ref[0, r0:r0 + C, cs:cs + LANES]], axis=0)
            rs.append(jnp.dot(lhs, s_scr[d, h].astype(BF16), preferred_element_type=F32))
        vbs = []
        for (d, h), r in zip(chains, rs):
            u_ref, w_ref, kdt_ref, qd_ref, qk_ref, eg_ref, o_ref = refs[d]
            r0 = (c if d == 0 else nch - 1 - c) * C
            cs = h * LANES
            vbs.append((u_ref[0, r0:r0 + C, cs:cs + LANES].astype(F32) - r[0:C]).astype(BF16))
        for (d, h), vb in zip(chains, vbs):
            u_ref, w_ref, kdt_ref, qd_ref, qk_ref, eg_ref, o_ref = refs[d]
            cc = c if d == 0 else nch - 1 - c
            lt, par = divmod(cc, 2)
            cs = h * LANES
            v_par = jnp.concatenate([vb, zc], axis=0) if par == 0 else jnp.concatenate([zc, vb], axis=0)
            kdt = kdt_ref[0, cs:cs + LANES, lt * LANES:(lt + 1) * LANES]
            eg = jnp.broadcast_to(eg_ref[0, cc, 0:1, d * H + h:d * H + h + 1], (LANES, LANES))
            s_scr[d, h] = s_scr[d, h] * eg + jnp.dot(kdt, v_par, preferred_element_type=F32)
        for d in range(2):
            u_ref, w_ref, kdt_ref, qd_ref, qk_ref, eg_ref, o_ref = refs[d]
            r0 = (c if d == 0 else nch - 1 - c) * C
            for p in range(H // 2):
                v0, v1 = vbs[d * H + 2 * p], vbs[d * H + 2 * p + 1]
                v_pair = jnp.concatenate([jnp.concatenate([v0, zc], axis=1), jnp.concatenate([zc, v1], axis=1)], axis=0)
                intra = jnp.dot(qk_ref[0, r0:r0 + C, p * LANES:(p + 1) * LANES], v_pair, preferred_element_type=F32)
                for hp in range(2):
                    h = 2 * p + hp
                    o = rs[d * H + h][C:2 * C] + intra[:, hp * LANES:(hp + 1) * LANES]
                    o_ref[0, r0:r0 + C, h * LANES:(h + 1) * LANES] = o.astype(BF16)

    @pl.when(i == n - 1)
    def _():
        sfin_ref[0] = s_scr[...]


def _chain_call(u, w, kdt, qd, qk, eg, s0, *, tc):
    b, l, _ = u.shape
    n = l // tc
    hw = GDN_HEADS * LANES
    fwd = lambda i, j: (i, j, 0)
    bwd = lambda i, j: (i, n - 1 - j, 1)
    bwd0 = lambda i, j: (i, n - 1 - j, 0)
    qkw = GDN_HEADS * CHUNK
    nchunk = tc // CHUNK
    in_specs = [
        pl.BlockSpec((1, tc, hw), fwd), pl.BlockSpec((1, tc, hw), fwd),
        pl.BlockSpec((1, hw, tc), lambda i, j: (i, 0, j)),
        pl.BlockSpec((1, tc, hw), fwd), pl.BlockSpec((1, tc, qkw), fwd),
        pl.BlockSpec((1, nchunk, 8, LANES), lambda i, j: (i, j, 0, 0)),
        pl.BlockSpec((1, tc, hw), bwd), pl.BlockSpec((1, tc, hw), bwd),
        pl.BlockSpec((1, hw, tc), lambda i, j: (i, 1, n - 1 - j)),
        pl.BlockSpec((1, tc, hw), bwd), pl.BlockSpec((1, tc, qkw), bwd),
        pl.BlockSpec((1, nchunk, 8, LANES), lambda i, j: (i, n - 1 - j, 0, 0)),
        pl.BlockSpec((1, 2, GDN_HEADS, LANES, LANES), lambda i, j: (i, 0, 0, 0, 0)),
    ]
    out_specs = [
        pl.BlockSpec((1, tc, hw), fwd),
        pl.BlockSpec((1, tc, hw), bwd0),
        pl.BlockSpec((1, 2, GDN_HEADS, LANES, LANES), lambda i, j: (i, 0, 0, 0, 0)),
    ]
    out_shape = [
        jax.ShapeDtypeStruct((b, l, hw), BF16),
        jax.ShapeDtypeStruct((b, l, hw), BF16),
        jax.ShapeDtypeStruct((b, 2, GDN_HEADS, LANES, LANES), F32),
    ]
    return pl.pallas_call(
        functools.partial(_chain_kernel, tc=tc),
        grid=(b, n),
        in_specs=in_specs,
        out_specs=out_specs,
        out_shape=out_shape,
        scratch_shapes=[pltpu.VMEM((2, GDN_HEADS, LANES, LANES), F32)],
        compiler_params=_cparams(("arbitrary", "arbitrary")),
        name="gdn_chain",
    )(u, w, kdt, qd, qk, eg, u, w, kdt, qd, qk, eg, s0)


def _attn_kernel(sink_ref, q_ref, kvp_ref, kvo_ref, kvn_ref, kvc_ref, zb_ref, o_ref, *, local):
    nb = pl.num_programs(1)
    n = pl.program_id(1)
    T = ATT_BLOCK
    q = q_ref[0]
    qs = jnp.concatenate([q[:, g * LANES:(g + 1) * LANES] for g in range(ATT_GROUP)], axis=0)
    lane = lax.broadcasted_iota(jnp.int32, (1, LANES), 1)
    rowi = lax.broadcasted_iota(jnp.int32, (ATT_GROUP * T, T), 0) % T
    colj = lax.broadcasted_iota(jnp.int32, (ATT_GROUP * T, T), 1)
    grp = lax.broadcasted_iota(jnp.int32, (ATT_GROUP * T, 1), 0) // T

    pieces = [(kvc_ref, None)]
    if local:
        pieces += [(kvp_ref, jnp.logical_and(colj >= rowi, n > 0)),
                   (kvo_ref, None),
                   (kvn_ref, jnp.logical_and(colj <= rowi, n < nb - 1))]

    def lane_tiles(a):
        return [a[:, c * LANES:(c + 1) * LANES] for c in range(a.shape[1] // LANES)]

    heads = range(ATT_KV_HEADS)
    sinks, scores = [], []
    for kvh in heads:
        kmask = ((lane // ROPE_F) % ATT_KV_HEADS == kvh)
        sink = jnp.zeros((ATT_GROUP * T, 1), F32)
        for g in range(ATT_GROUP):
            sink = jnp.where(grp == g, sink_ref[kvh * ATT_GROUP + g], sink)
        sinks.append(sink)
        sc = []
        for ref, valid in pieces:
            kv = ref[0]
            kh = jnp.where(kmask, kv[:, 0:LANES], jnp.zeros_like(kv[:, 0:LANES]))
            s = lax.dot_general(qs, kh, (((1,), (1,)), ((), ())), preferred_element_type=F32)
            if valid is not None:
                s = jnp.where(valid, s, NEG_INF)
            sc.append(s)
        scores.append(sc)
    ms = []
    for kvh in heads:
        tiles = [t for s in scores[kvh] for t in lane_tiles(s)]
        mt = tiles[0]
        for t in tiles[1:]:
            mt = jnp.maximum(mt, t)
        ms.append(jnp.maximum(sinks[kvh], jnp.max(mt, axis=-1, keepdims=True)))
    ps, denoms = [], []
    for kvh in heads:
        pk = [jnp.exp(s - ms[kvh]) for s in scores[kvh]]
        tiles = [t for p in pk for t in lane_tiles(p)]
        st = tiles[0]
        for t in tiles[1:]:
            st = st + t
        denoms.append(jnp.exp(sinks[kvh] - ms[kvh]) + jnp.sum(st, axis=-1, keepdims=True))
        ps.append([p.astype(BF16) for p in pk])
    acc = jnp.zeros((ATT_GROUP * T, LANES), F32)
    for kvh in heads:
        vmask = (lane // ATT_HD == kvh)
        o_h = jnp.zeros((ATT_GROUP * T, LANES), F32)
        for (ref, _), p in zip(pieces, ps[kvh]):
            kv = ref[0]
            vh = jnp.where(vmask, kv[:, LANES:2 * LANES], jnp.zeros_like(kv[:, 0:LANES]))
            o_h = o_h + jnp.dot(p, vh, preferred_element_type=F32)
        acc = acc + o_h / denoms[kvh]
    for g in range(ATT_GROUP):
        og = acc[g * T:(g + 1) * T, :] * zb_ref[0, :, g * LANES:(g + 1) * LANES].astype(F32)
        o_ref[0, :, g * LANES:(g + 1) * LANES] = og.astype(BF16)


def _attn_call(sink, q, kv, kvc, zb, *, local):
    b, l, _ = q.shape
    nb = l // ATT_BLOCK
    lc = kvc.shape[1]
    blk = lambda f: pl.BlockSpec((1, ATT_BLOCK, 2 * LANES), f)
    return pl.pallas_call(
        functools.partial(_attn_kernel, local=local),
        grid=(b, nb),
        in_specs=[
            pl.BlockSpec(memory_space=pltpu.SMEM),
            pl.BlockSpec((1, ATT_BLOCK, ATT_WIDTH), lambda i, j: (i, j, 0)),
            blk(lambda i, j: (i, jnp.maximum(j - 1, 0), 0)),
            blk(lambda i, j: (i, j, 0)),
            blk(lambda i, j: (i, jnp.minimum(j + 1, nb - 1), 0)),
            pl.BlockSpec((1, lc, 2 * LANES), lambda i, j: (i, 0, 0)),
            pl.BlockSpec((1, ATT_BLOCK, ATT_WIDTH), lambda i, j: (i, j, 0)),
        ],
        out_specs=pl.BlockSpec((1, ATT_BLOCK, ATT_WIDTH), lambda i, j: (i, j, 0)),
        out_shape=jax.ShapeDtypeStruct((b, l, ATT_WIDTH), BF16),
        compiler_params=_cparams(("arbitrary", "arbitrary")),
        name="attn_local" if local else "attn_ctx",
    )(sink, q, kv, kv, kv, kvc, zb)


def _merge_kernel(x_ref, gate_ref, of_ref, ob_ref, za_ref, yb_ref, ga_ref, gb_ref, gnw_ref,
                  wpa_ref, wpb_ref, wout_ref, o_ref):
    gnw = gnw_ref[...]
    ya_parts = []
    for h in range(GDN_HEADS):
        sl = slice(h * LANES, (h + 1) * LANES)
        o = of_ref[0, :, sl].astype(F32) + ob_ref[0, :, sl].astype(F32)
        on = o * lax.rsqrt(jnp.mean(o * o, axis=-1, keepdims=True) + EPS) * gnw
        ya_parts.append((on * za_ref[0, :, sl].astype(F32)).astype(BF16))
    ya = jnp.concatenate(ya_parts, axis=1)
    pa = jnp.dot(ya, wpa_ref[...], preferred_element_type=F32)
    pb = jnp.dot(yb_ref[0], wpb_ref[...], preferred_element_type=F32)
    y = ga_ref[0].astype(F32) * pa + gb_ref[0].astype(F32) * pb
    out = jnp.dot(y.astype(BF16), wout_ref[...], preferred_element_type=F32)
    o_ref[0] = x_ref[0] + gate_ref[0] * out


def _merge_call(x, gate, of, ob, za, yb, ga, gb, gnw, wpa, wpb, wout, *, tm):
    b, l, _ = x.shape
    per_batch_mod = gate.shape[0] == b and b > 1
    mod_map = (lambda i, j: (i, 0, 0)) if per_batch_mod else (lambda i, j: (0, 0, 0))
    row = lambda w: pl.BlockSpec((1, tm, w), lambda i, j: (i, j, 0))
    const2 = lambda i, j: (0, 0)
    return pl.pallas_call(
        _merge_kernel,
        grid=(b, l // tm),
        in_specs=[
            row(D_MODEL), pl.BlockSpec((1, 1, D_MODEL), mod_map),
            row(GDN_WIDTH), row(GDN_WIDTH), row(GDN_WIDTH), row(ATT_WIDTH), row(D_MODEL), row(D_MODEL),
            pl.BlockSpec((1, LANES), const2),
            pl.BlockSpec((GDN_WIDTH, D_MODEL), const2),
            pl.BlockSpec((ATT_WIDTH, D_MODEL), const2),
            pl.BlockSpec((D_MODEL, D_MODEL), const2),
        ],
        out_specs=row(D_MODEL),
        out_shape=jax.ShapeDtypeStruct((b, l, D_MODEL), F32),
        compiler_params=_cparams(("arbitrary", "arbitrary")),
        name="merge",
    )(x, gate, of, ob, za, yb, ga, gb, gnw, wpa, wpb, wout)


def _pick_tile(l, pref):
    t = min(pref, l)
    while l % t:
        t //= 2
    return t


def kernel(x, c, ctx, c_ctx, norm_w, w_mod, b_mod, w_in, conv_w, a_log, dt_bias, gdn_norm_w,
           q_norm_w, k_norm_w, sink, w_proj_a, w_proj_b, w_out):
    b, l, _ = x.shape
    lc = ctx.shape[1]
    assert l % ATT_BLOCK == 0 and lc % ATT_BLOCK == 0 and x.shape[2] == D_MODEL

    rows = ((b + 1 + 7) // 8) * 8
    c_all = jnp.concatenate([c, c_ctx[None, :], jnp.zeros((rows - b - 1, D_MODEL), F32)], axis=0)
    mod = _mod_call(c_all, w_mod, b_mod)

    cos_l, sin_l = _rope_tables(l)
    cos_c = jnp.ones((lc, LANES), F32)
    sin_c = jnp.zeros((lc, LANES), F32)
    grp = (np.arange(LANES) // ROPE_F) % ATT_KV_HEADS
    gmat = jnp.asarray((grp[:, None] == grp[None, :]).astype(np.float32), dtype=BF16)
    s0 = jnp.zeros((b, 2, GDN_HEADS, LANES, LANES), F32)
    tm_l = _pick_tile(l, 512)
    tm_c = _pick_tile(lc, 256)
    tc_l = _pick_tile(l, 256)
    tc_c = _pick_tile(lc, 256)
    tch_l = _pick_tile(l, 512)

    w_packed_all = _pack_w_in(w_in)
    ng = 2 * GDN_HEADS
    gpad = jnp.zeros((DEPTH, ng), F32)
    gvec_all = jnp.concatenate([
        jnp.concatenate([gpad, jnp.exp(a_log.astype(F32)).reshape(DEPTH, ng),
                         jnp.zeros((DEPTH, LANES - 2 * ng), F32)], axis=1)[:, None, :],
        jnp.concatenate([gpad, dt_bias.astype(F32).reshape(DEPTH, ng),
                         jnp.zeros((DEPTH, LANES - 2 * ng), F32)], axis=1)[:, None, :],
        jnp.zeros((DEPTH, 6, LANES), F32)], axis=1)
    qn_all = jax.vmap(_norm_lanes)(q_norm_w) * (ATT_HD ** -0.5)
    kn_all = jax.vmap(_norm_lanes)(k_norm_w)
    cw_all = jnp.pad(conv_w, ((0, 0), (0, 8 - SHORT_CONV), (0, 0)))
    wpa_all = w_proj_a.astype(BF16)
    wpb_all = jnp.swapaxes(w_proj_b.reshape(DEPTH, ATT_KV_HEADS, ATT_GROUP, ATT_HD, D_MODEL), 1, 2).reshape(
        DEPTH, ATT_WIDTH, D_MODEL).astype(BF16)
    wout_all = w_out.astype(BF16)

    for i in range(DEPTH):
        update_ctx = i < DEPTH - 1
        shift, scale, gate = (mod[i, :, k * D_MODEL:(k + 1) * D_MODEL] for k in range(3))
        shift_l, scale_l, gate_l = (t[:b, None, :] for t in (shift, scale, gate))
        shift_c, scale_c, gate_c = (t[b:b + 1, None, :] for t in (shift, scale, gate))
        w_packed = w_packed_all[i]
        normw = norm_w[i][None, :]
        gvec = gvec_all[i]
        qn = qn_all[i][None, :]
        kn = kn_all[i][None, :]
        cw = cw_all[i]
        gnw = gdn_norm_w[i][None, :]
        wpa = wpa_all[i]
        wpb = wpb_all[i]
        wout = wout_all[i]

        outs_c = _proj_call(ctx, scale_c, shift_c, normw, w_packed, gvec, qn, kn, gmat, cos_c, sin_c,
                            with_rest=update_ctx, tm=tm_c)
        qkv_c, gates_c, kv_c = outs_c[:3]
        prep_c = _prep_call(qkv_c, cw, gates_c, tc=tc_c)
        of_c, ob_c, s_ctx = _chain_call(*prep_c, s0, tc=tc_c)

        qkv_l, gates_l, kv_l, za_l, q_l, zb_l, ga_l, gb_l = _proj_call(
            x, scale_l, shift_l, normw, w_packed, gvec, qn, kn, gmat, cos_l, sin_l, with_rest=True, tm=tm_l)
        prep_l = _prep_call(qkv_l, cw, gates_l, tc=tc_l)
        of_l, ob_l, _ = _chain_call(*prep_l, s_ctx, tc=tch_l)
        yb_l = _attn_call(sink[i], q_l, kv_l, kv_c, zb_l, local=True)
        x_new = _merge_call(x, gate_l, of_l, ob_l, za_l, yb_l, ga_l, gb_l, gnw, wpa, wpb, wout, tm=tm_l)

        if update_ctx:
            za_c, q_c, zb_c, ga_c, gb_c = outs_c[3:]
            yb_c = _attn_call(sink[i], q_c, kv_c, kv_c, zb_c, local=False)
            ctx = _merge_call(ctx, gate_c, of_c, ob_c, za_c, yb_c, ga_c, gb_c, gnw, wpa, wpb, wout, tm=tm_c)
        x = x_new
    return x
```

```python
import functools
import math

import numpy as np
import jax
import jax.numpy as jnp
from jax import lax
from jax.experimental import pallas as pl
from jax.experimental.pallas import tpu as pltpu

F32 = jnp.float32
BF16 = jnp.bfloat16

D_MODEL = 1024
DEPTH = 2
GRID_W = 64
EPS = 1e-6
NEG_INF = -1e30

GDN_HEADS = 4
GDN_DK = 128
GDN_DV = 128
GDN_QKV = GDN_HEADS * (2 * GDN_DK + GDN_DV)
GDN_WIDTH = GDN_HEADS * GDN_DV
SHORT_CONV = 5
CHUNK = 64

ATT_HEADS = 8
ATT_KV_HEADS = 2
ATT_GROUP = ATT_HEADS // ATT_KV_HEADS
ATT_HD = 64
ATT_WIDTH = ATT_HEADS * ATT_HD
ATT_KV_WIDTH = ATT_KV_HEADS * ATT_HD
ATT_BLOCK = 128
ROPE_BASE = 10000.0
AXIS_DIM = ATT_HD // 2
ROPE_F = AXIS_DIM // 2
LOG2E = math.log2(math.e)

N_STATE = GDN_QKV + 4 * GDN_HEADS + 2 * ATT_KV_WIDTH

LANES = 128
P_QKV = 0
P_GATE = GDN_QKV
P_K = P_GATE + LANES
P_V = P_K + ATT_KV_WIDTH
P_STATE_END = P_V + ATT_KV_WIDTH
P_ZA = P_STATE_END
P_Q = P_ZA + GDN_WIDTH
P_ZB = P_Q + ATT_WIDTH
P_GA = P_ZB + ATT_WIDTH
P_GB = P_GA + D_MODEL
P_END = P_GB + D_MODEL

VMEM_LIMIT = 56 * 1024 * 1024


def _cparams(sem):
    return pltpu.CompilerParams(dimension_semantics=sem, vmem_limit_bytes=VMEM_LIMIT)


def _k_cols(w):
    lead = w.shape[:-1]
    w = w.reshape(lead + (ATT_KV_HEADS, 4, ROPE_F))
    return jnp.swapaxes(w, -3, -2).reshape(lead + (ATT_KV_WIDTH,))


def _q_cols(w):
    lead = w.shape[:-1]
    w = w.reshape(lead + (ATT_KV_HEADS, ATT_GROUP, 4, ROPE_F))
    return jnp.moveaxis(w, -4, -2).reshape(lead + (ATT_WIDTH,))


def _o_cols(w):
    lead = w.shape[:-1]
    w = w.reshape(lead + (ATT_KV_HEADS, ATT_GROUP, ATT_HD))
    return jnp.swapaxes(w, -3, -2).reshape(lead + (ATT_WIDTH,))


def _norm_lanes(w):
    w = w.reshape(4, 1, ROPE_F)
    return jnp.broadcast_to(w, (4, ATT_KV_HEADS, ROPE_F)).reshape(LANES)


def _pack_w_in(w_in):
    w_in = w_in.astype(BF16)
    o = 0
    qkv = w_in[..., o:o + GDN_QKV]; o += GDN_QKV
    gates = w_in[..., o:o + 4 * GDN_HEADS]; o += 4 * GDN_HEADS
    kb = w_in[..., o:o + ATT_KV_WIDTH]; o += ATT_KV_WIDTH
    vb = w_in[..., o:o + ATT_KV_WIDTH]; o += ATT_KV_WIDTH
    za = w_in[..., o:o + GDN_WIDTH]; o += GDN_WIDTH
    qb = w_in[..., o:o + ATT_WIDTH]; o += ATT_WIDTH
    zb = w_in[..., o:o + ATT_WIDTH]; o += ATT_WIDTH
    ga = w_in[..., o:o + D_MODEL]; o += D_MODEL
    gb = w_in[..., o:o + D_MODEL]
    gates = jnp.concatenate([gates, jnp.zeros(gates.shape[:-1] + (LANES - 4 * GDN_HEADS,), BF16)], axis=-1)
    return jnp.concatenate(
        [qkv, gates, _k_cols(kb), vb, za, _q_cols(qb), _o_cols(zb), ga, gb], axis=-1)


def _rope_tables(seq_len):
    t = jnp.arange(seq_len, dtype=F32)
    pos_row = jnp.floor(t / GRID_W)
    pos_col = t - pos_row * GRID_W
    inv = ROPE_BASE ** (-jnp.arange(0, AXIS_DIM, 2, dtype=F32) / AXIS_DIM)
    ang_r = pos_row[:, None] * inv[None, :]
    ang_c = pos_col[:, None] * inv[None, :]

    def lay(a):
        return jnp.concatenate([a, a], axis=1)

    cos = jnp.concatenate([lay(jnp.cos(ang_r)), lay(jnp.cos(ang_r)), lay(jnp.cos(ang_c)), lay(jnp.cos(ang_c))], axis=1)
    sin = jnp.concatenate([-lay(jnp.sin(ang_r)), lay(jnp.sin(ang_r)), -lay(jnp.sin(ang_c)), lay(jnp.sin(ang_c))], axis=1)
    return cos, sin


def _mod_kernel(c_ref, w_ref, b_ref, o_ref):
    c = c_ref[...]
    s = c * jax.nn.sigmoid(c)
    o_ref[0, 0] = jnp.dot(s.astype(BF16), w_ref[0].astype(BF16), preferred_element_type=F32) + b_ref[0]


def _mod_call(c_all, w_mod, b_mod):
    rows = c_all.shape[0]
    nblk = 3
    return pl.pallas_call(
        _mod_kernel,
        grid=(DEPTH, nblk),
        in_specs=[
            pl.BlockSpec((rows, D_MODEL), lambda l, j: (0, 0)),
            pl.BlockSpec((1, D_MODEL, D_MODEL), lambda l, j: (l, 0, j)),
            pl.BlockSpec((1, 1, D_MODEL), lambda l, j: (l, 0, j)),
        ],
        out_specs=pl.BlockSpec((1, 1, rows, D_MODEL), lambda l, j: (l, j, 0, 0)),
        out_shape=jax.ShapeDtypeStruct((DEPTH, nblk, rows, D_MODEL), F32),
        compiler_params=_cparams(("arbitrary", "arbitrary")),
        name="mod",
    )(c_all, w_mod, b_mod.reshape(DEPTH, 1, 3 * D_MODEL))


def _mod_spec(layer, kind, rows, row):
    base = (layer * 3 + kind) * rows
    if row is None:
        return pl.BlockSpec((1, 1, D_MODEL), lambda i, j: (base + i, 0, 0))
    return pl.BlockSpec((1, 1, D_MODEL), lambda i, j: (base + row, 0, 0))


def _layer_spec(layer, *block):
    zeros = (0,) * len(block)
    return pl.BlockSpec((None,) + tuple(block), lambda i, j: (layer,) + zeros)


def _swap32(t):
    lane = lax.broadcasted_iota(jnp.int32, t.shape, 1)
    even = (lane // 32) % 2 == 0
    return jnp.where(even, pltpu.roll(t, 96, 1), pltpu.roll(t, 32, 1))


def _headnorm_rope(parts, nws, gmat, cos, sin):
    t = jnp.concatenate(parts, axis=0) if len(parts) > 1 else parts[0]
    ss = jnp.dot((t * t).astype(BF16), gmat, preferred_element_type=F32)
    inv = lax.rsqrt(ss * (1.0 / ATT_HD) + EPS)
    tm = parts[0].shape[0]
    outs = []
    for i, nw in enumerate(nws):
        tn = parts[i] * inv[i * tm:(i + 1) * tm] * nw
        outs.append(tn * cos + _swap32(tn) * sin)
    return outs


def _softplus(x):
    return jnp.maximum(x, 0.0) + jnp.log1p(jnp.exp(-jnp.abs(x)))


def _proj_kernel(x_ref, scale_ref, shift_ref, normw_ref, w_ref, gvec_ref, qn_ref, kn_ref, gmat_ref,
                 cos_ref, sin_ref, *out_refs, with_rest):
    if with_rest:
        qkv_ref, gates_ref, kv_ref, za_ref, q_ref, zb_ref, ga_ref, gb_ref = out_refs
    else:
        qkv_ref, gates_ref, kv_ref = out_refs
    x = x_ref[0]
    ms = jnp.mean(x * x, axis=-1, keepdims=True)
    a = normw_ref[...] * (1.0 + scale_ref[0])
    h = (x * lax.rsqrt(ms + EPS) * a + shift_ref[0]).astype(BF16)

    def proj(lo, width):
        return jnp.dot(h, w_ref[:, lo:lo + width], preferred_element_type=F32)

    for c in range(GDN_QKV // 512):
        qkv_ref[0, :, c * 512:(c + 1) * 512] = proj(P_QKV + c * 512, 512).astype(BF16)

    g = proj(P_GATE, LANES)
    lane = lax.broadcasted_iota(jnp.int32, g.shape, 1)
    beta = jax.nn.sigmoid(g)
    dec = -gvec_ref[0:1, :] * _softplus(g + gvec_ref[1:2, :])
    gates_ref[0] = jnp.where(lane < 2 * GDN_HEADS, beta, dec)

    gmat = gmat_ref[...]
    cos = cos_ref[...]
    sin = sin_ref[...]
    kv_ref[0, :, LANES:2 * LANES] = proj(P_V, LANES).astype(BF16)
    parts = [proj(P_K, LANES)]
    nws = [kn_ref[...]]
    if with_rest:
        qraw = proj(P_Q, ATT_WIDTH)
        parts += [qraw[:, gi * LANES:(gi + 1) * LANES] for gi in range(ATT_GROUP)]
        nws += [qn_ref[...]] * ATT_GROUP
    normed = _headnorm_rope(parts, nws, gmat, cos, sin)
    kv_ref[0, :, 0:LANES] = normed[0].astype(BF16)

    if with_rest:
        for gi in range(ATT_GROUP):
            q_ref[0, :, gi * LANES:(gi + 1) * LANES] = normed[1 + gi].astype(BF16)
        z = proj(P_ZA, GDN_WIDTH)
        za_ref[0] = (z * jax.nn.sigmoid(z)).astype(BF16)
        z = proj(P_ZB, ATT_WIDTH)
        zb_ref[0] = (z * jax.nn.sigmoid(z)).astype(BF16)
        for c in range(D_MODEL // 512):
            ga_ref[0, :, c * 512:(c + 1) * 512] = jax.nn.sigmoid(proj(P_GA + c * 512, 512)).astype(BF16)
            gb_ref[0, :, c * 512:(c + 1) * 512] = jax.nn.sigmoid(proj(P_GB + c * 512, 512)).astype(BF16)


def _proj_call(x, mod, mod_row, layer, normw, w_packed, gvec, qn, kn, gmat, cos, sin, *, with_rest, tm):
    b, l, _ = x.shape
    nw = P_END if with_rest else P_STATE_END
    rows = mod.shape[0] // (3 * DEPTH)
    const2 = lambda i, j: (0, 0)
    row_map = lambda i, j: (i, j, 0)
    in_specs = [
        pl.BlockSpec((1, tm, D_MODEL), row_map),
        _mod_spec(layer, 1, rows, mod_row),
        _mod_spec(layer, 0, rows, mod_row),
        _layer_spec(layer, 1, D_MODEL),
        _layer_spec(layer, D_MODEL, nw),
        _layer_spec(layer, 8, LANES),
        _layer_spec(layer, 1, LANES),
        _layer_spec(layer, 1, LANES),
        pl.BlockSpec((LANES, LANES), const2),
        pl.BlockSpec((tm, LANES), lambda i, j: (j, 0)),
        pl.BlockSpec((tm, LANES), lambda i, j: (j, 0)),
    ]
    widths = [(GDN_QKV, BF16), (LANES, F32), (2 * LANES, BF16)]
    if with_rest:
        widths += [(GDN_WIDTH, BF16), (ATT_WIDTH, BF16), (ATT_WIDTH, BF16), (D_MODEL, BF16), (D_MODEL, BF16)]
    out_specs = [pl.BlockSpec((1, tm, w), row_map) for w, _ in widths]
    out_shape = [jax.ShapeDtypeStruct((b, l, w), dt) for w, dt in widths]
    return pl.pallas_call(
        functools.partial(_proj_kernel, with_rest=with_rest),
        grid=(b, l // tm),
        in_specs=in_specs,
        out_specs=out_specs,
        out_shape=out_shape,
        compiler_params=_cparams(("arbitrary", "arbitrary")),
        name="proj_full" if with_rest else "proj_state",
    )(x, mod, mod, normw, w_packed, gvec, qn, kn, gmat, cos, sin)


HALO = 16


def _blockdiag(y, isf):
    zero = jnp.zeros_like(y)
    return jnp.concatenate([jnp.where(isf, y, zero), jnp.where(isf, zero, y)], axis=0)


def _short_conv_stage(raw_ref, prev_ref, next_ref, cw_ref, raw_scr, qkv_scr, tc):
    j = pl.program_id(1)
    n = pl.num_programs(1)
    half = SHORT_CONV // 2
    for cb in range(GDN_QKV // LANES):
        sl = slice(cb * LANES, (cb + 1) * LANES)
        raw_scr[cb, 0:HALO, :] = jnp.where(j > 0, prev_ref[0, :, sl].astype(F32), 0.0)
        raw_scr[cb, HALO:HALO + tc, :] = raw_ref[0, :, sl].astype(F32)
        raw_scr[cb, HALO + tc:2 * HALO + tc, :] = jnp.where(j < n - 1, next_ref[0, :, sl].astype(F32), 0.0)
        y = raw_scr[cb, HALO - half:HALO - half + tc, :] * cw_ref[0:1, sl]
        for tap in range(1, SHORT_CONV):
            y = y + raw_scr[cb, HALO - half + tap:HALO - half + tap + tc, :] * cw_ref[tap:tap + 1, sl]
        y = y * jax.nn.sigmoid(y)
        if cb < 2 * GDN_HEADS:
            scale = GDN_DK ** -0.5 if cb < GDN_HEADS else 1.0
            y = y * (lax.rsqrt(jnp.sum(y * y, axis=-1, keepdims=True) + EPS) * scale)
        qkv_scr[:, sl] = y.astype(BF16)


def _prep_kernel(raw_ref, prev_ref, next_ref, cw_ref, gates_ref, u_ref, w_ref, kdt_ref, qd_ref, qk_ref, eg_ref,
                 raw_scr, qkv_ref, a_scr, t_scr, rhs_scr, *, tc):
    H = GDN_HEADS
    C = CHUNK

    @pl.when(jnp.logical_and(pl.program_id(0) == 0, pl.program_id(1) == 0))
    def _():
        rhs_scr[...] = jnp.zeros(rhs_scr.shape, BF16)

    _short_conv_stage(raw_ref, prev_ref, next_ref, cw_ref, raw_scr, qkv_ref, tc)
    G = gates_ref[0]
    gT = G.T[2 * H:4 * H, :]
    lane_t = lax.broadcasted_iota(jnp.int32, gT.shape, 1)
    pos = lane_t % C
    pre = gT
    suf = gT
    for s in (1, 2, 4, 8, 16, 32):
        pre = pre + jnp.where(pos >= s, pltpu.roll(pre, s, 1), 0.0)
        suf = suf + jnp.where(pos < C - s, pltpu.roll(suf, tc - s, 1), 0.0)
    row8 = lax.broadcasted_iota(jnp.int32, gT.shape, 0)
    gcT = jnp.where(row8 < H, pre, suf)
    gc = jnp.concatenate([gcT, jnp.zeros((LANES - 2 * H, tc), F32)], axis=0).T

    lane = lax.broadcasted_iota(jnp.int32, (C, LANES), 1)
    ii = lax.broadcasted_iota(jnp.int32, (C, LANES), 0)
    jj = lane % C
    isf = lane < C
    isb = jnp.logical_not(isf)
    incl = jnp.logical_or(jnp.logical_and(isf, ii >= jj), jnp.logical_and(isb, ii <= jj))
    strict = jnp.logical_or(jnp.logical_and(isf, ii > jj), jnp.logical_and(isb, ii < jj))
    eye2 = (ii == jj).astype(F32)
    same = {kk: (ii // kk) == (jj // kk) for kk in (2, 4, 8, 16, 32, 64)}
    lane8 = lax.broadcasted_iota(jnp.int32, (1, LANES), 1)

    def colb(arr, r0, c):
        return jnp.broadcast_to(arr[r0:r0 + C, c:c + 1], (C, LANES))

    for s in range(tc // C):
        r0 = s * C
        glrow = jnp.where(lane8 < H, gc[r0 + C - 1:r0 + C, :], gc[r0:r0 + 1, :])
        eg_ref[0, s] = jnp.broadcast_to(jnp.exp(glrow), (8, LANES))

    items = [(lt, 2 * p + hp, halfsel) for lt in range(tc // LANES) for p in range(H // 2)
             for halfsel in range(2) for hp in range(2)]
    gts = {}
    for lt in range(tc // LANES):
        gt = gcT[:, lt * LANES:(lt + 1) * LANES]
        gts[lt] = (gt, pltpu.roll(gt, C, 1))

    kd_parts = {}
    qk_hold = {}

    def stage1(idx):
        lt, h, halfsel = items[idx]
        gt, gt_r = gts[lt]
        r0 = (2 * lt + halfsel) * C
        if halfsel == 0:
            row_f, row_b = gt[h:h + 1, :], gt_r[H + h:H + h + 1, :]
        else:
            row_f, row_b = gt_r[h:h + 1, :], gt[H + h:H + h + 1, :]
        gc_row2 = jnp.broadcast_to(jnp.where(lane8 < C, row_f, row_b), (C, LANES))
        gcf = colb(gc, r0, h)
        gcb = colb(gc, r0, H + h)
        dec = jnp.where(incl, jnp.exp(jnp.where(incl, jnp.where(isf, gcf, gcb) - gc_row2, 0.0)), 0.0)
        bf = colb(G, r0, h)
        bb = colb(G, r0, H + h)
        qt = qkv_ref[r0:r0 + C, h * LANES:(h + 1) * LANES]
        kt = qkv_ref[r0:r0 + C, (H + h) * LANES:(H + h + 1) * LANES]
        vt = qkv_ref[r0:r0 + C, (2 * H + h) * LANES:(2 * H + h + 1) * LANES]
        kq = lax.dot_general(jnp.concatenate([kt, qt], axis=0), jnp.concatenate([kt, kt], axis=0),
                             (((1,), (1,)), ((), ())), preferred_element_type=F32)
        a2 = jnp.where(strict, jnp.where(isf, bf, bb) * kq[0:C] * dec, 0.0)
        a_scr[idx] = a2
        t_scr[idx] = eye2 - jnp.where(same[2], a2, 0.0)
        qk2 = kq[C:2 * C] * dec
        if h % 2 == 0:
            qk_hold[(lt, h, halfsel)] = qk2
        else:
            qk_even = qk_hold.pop((lt, h - 1, halfsel))
            pc = (h // 2) * LANES
            qk_ref[0, r0:r0 + C, pc:pc + LANES] = jnp.where(isf, qk_even, pltpu.roll(qk2, C, 1)).astype(BF16)
            qk_ref[0, r0:r0 + C, H * C + pc:H * C + pc + LANES] = jnp.where(
                isf, pltpu.roll(qk_even, C, 1), qk2).astype(BF16)

        kf = kt.astype(F32)
        vf = vt.astype(F32)
        qf = qt.astype(F32)
        egf = jnp.exp(gcf)
        egb = jnp.exp(gcb)
        rhs_scr[idx, 0:C, 0:2 * LANES] = jnp.concatenate([vf * bf, kf * (bf * egf)], axis=1).astype(BF16)
        rhs_scr[idx, C:2 * C, 2 * LANES:4 * LANES] = jnp.concatenate([vf * bb, kf * (bb * egb)], axis=1).astype(BF16)
        cf = h * LANES
        cb = (H + h) * LANES
        qd_ref[0, r0:r0 + C, cf:cf + LANES] = (qf * egf).astype(BF16)
        qd_ref[0, r0:r0 + C, cb:cb + LANES] = (qf * egb).astype(BF16)
        glf = jnp.broadcast_to(gc[r0 + C - 1:r0 + C, h:h + 1], (C, LANES))
        glb = jnp.broadcast_to(gc[r0:r0 + 1, H + h:H + h + 1], (C, LANES))
        kd_parts[(lt, h, halfsel)] = (kf * jnp.exp(glf - gcf), kf * jnp.exp(glb - gcb))
        if halfsel == 1:
            for d in range(2):
                kdt = jnp.concatenate([kd_parts[(lt, h, 0)][d], kd_parts[(lt, h, 1)][d]], axis=0).T
                kdt_ref[0, (d * H + h) * LANES:(d * H + h + 1) * LANES, lt * LANES:(lt + 1) * LANES] = kdt.astype(BF16)

    levels = (2, 4, 8, 16, 32)

    def round_a(kk, idxs):
        emask = jnp.logical_and(same[2 * kk], jnp.logical_not(same[kk]))
        ps = []
        for idx in idxs:
            e = jnp.where(emask, a_scr[idx], 0.0).astype(BF16)
            ps.append(jnp.dot(e, _blockdiag(t_scr[idx].astype(BF16), isf), preferred_element_type=F32))
        return ps

    def round_b(idxs, ps):
        for idx, p in zip(idxs, ps):
            t2 = t_scr[idx]
            t_scr[idx] = t2 - jnp.dot(t2.astype(BF16), _blockdiag(p.astype(BF16), isf),
                                      preferred_element_type=F32)

    def stage3(idx):
        lt, h, halfsel = items[idx]
        r0 = (2 * lt + halfsel) * C
        uw = jnp.dot(t_scr[idx].astype(BF16), rhs_scr[idx], preferred_element_type=F32)
        cf = h * LANES
        cb = (H + h) * LANES
        u_ref[0, r0:r0 + C, cf:cf + LANES] = uw[:, 0:LANES].astype(BF16)
        w_ref[0, r0:r0 + C, cf:cf + LANES] = uw[:, LANES:2 * LANES].astype(BF16)
        u_ref[0, r0:r0 + C, cb:cb + LANES] = uw[:, 2 * LANES:3 * LANES].astype(BF16)
        w_ref[0, r0:r0 + C, cb:cb + LANES] = uw[:, 3 * LANES:4 * LANES].astype(BF16)

    everything = list(range(len(items)))
    for idx in everything:
        stage1(idx)
    for kk in levels:
        round_b(everything, round_a(kk, everything))
    for idx in everything:
        stage3(idx)


def _prep_call(qkv_raw, conv_w, layer, gates, *, tc):
    b, l, _ = qkv_raw.shape
    row_map = lambda i, j: (i, j, 0)
    wide = 2 * GDN_HEADS * LANES
    n_items = (tc // CHUNK) * GDN_HEADS
    hpt = tc // HALO
    nhalo = l // HALO
    n = l // tc
    return pl.pallas_call(
        functools.partial(_prep_kernel, tc=tc),
        grid=(b, l // tc),
        in_specs=[
            pl.BlockSpec((1, tc, GDN_QKV), row_map),
            pl.BlockSpec((1, HALO, GDN_QKV), lambda i, j: (i, jnp.maximum(j * hpt - 1, 0), 0)),
            pl.BlockSpec((1, HALO, GDN_QKV), lambda i, j: (i, jnp.minimum((j + 1) * hpt, nhalo - 1), 0)),
            _layer_spec(layer, 8, GDN_QKV),
            pl.BlockSpec((1, tc, LANES), row_map),
        ],
        out_specs=[
            pl.BlockSpec((1, tc, wide), row_map),
            pl.BlockSpec((1, tc, wide), row_map),
            pl.BlockSpec((1, wide, tc), lambda i, j: (i, 0, j)),
            pl.BlockSpec((1, tc, wide), row_map),
            pl.BlockSpec((1, tc, 2 * GDN_HEADS * CHUNK), row_map),
            pl.BlockSpec((1, tc // CHUNK, 8, LANES), lambda i, j: (i, j, 0, 0)),
        ],
        out_shape=[
            jax.ShapeDtypeStruct((b, l, wide), BF16),
            jax.ShapeDtypeStruct((b, l, wide), BF16),
            jax.ShapeDtypeStruct((b, wide, l), BF16),
            jax.ShapeDtypeStruct((b, l, wide), BF16),
            jax.ShapeDtypeStruct((b, l, 2 * GDN_HEADS * CHUNK), BF16),
            jax.ShapeDtypeStruct((b, l // CHUNK, 8, LANES), F32),
        ],
        scratch_shapes=[
            pltpu.VMEM((GDN_QKV // LANES, tc + 2 * HALO, LANES), F32),
            pltpu.VMEM((tc, GDN_QKV), BF16),
            pltpu.VMEM((n_items, CHUNK, LANES), F32),
            pltpu.VMEM((n_items, CHUNK, LANES), F32),
            pltpu.VMEM((n_items, 2 * CHUNK, 4 * LANES), BF16),
        ],
        compiler_params=_cparams(("arbitrary", "arbitrary")),
        name="gdn_prep",
    )(qkv_raw, qkv_raw, qkv_raw, conv_w, gates)


def _chain_kernel(uf_ref, wf_ref, kdtf_ref, qdf_ref, qkf_ref, egf_ref,
                  ub_ref, wb_ref, kdtb_ref, qdb_ref, qkb_ref, egb_ref, s0_ref,
                  of_ref, ob_ref, sfin_ref, s_scr, *, tc):
    H = GDN_HEADS
    C = CHUNK
    i = pl.program_id(1)
    n = pl.num_programs(1)

    @pl.when(i == 0)
    def _():
        s_scr[...] = s0_ref[0]

    nch = tc // C
    zc = jnp.zeros((C, LANES), BF16)
    refs = ((uf_ref, wf_ref, kdtf_ref, qdf_ref, qkf_ref, egf_ref, of_ref),
            (ub_ref, wb_ref, kdtb_ref, qdb_ref, qkb_ref, egb_ref, ob_ref))
    chains = [(d, h) for d in range(2) for h in range(H)]
    for c in range(nch):
        rs = []
        for d, h in chains:
            u_ref, w_ref, kdt_ref, qd_ref, qk_ref, eg_ref, o_ref = refs[d]
            r0 = (c if d == 0 else nch - 1 - c) * C
            cs = h * LANES
            lhs = jnp.concatenate([w_ref[0, r0:r0 + C, cs:cs + LANES], qd_ref[0, r0:r0 + C, cs:cs + LANES]], axis=0)
            rs.append(jnp.dot(lhs, s_scr[d, h].astype(BF16), preferred_element_type=F32))
        vbs = []
        for (d, h), r in zip(chains, rs):
            u_ref, w_ref, kdt_ref, qd_ref, qk_ref, eg_ref, o_ref = refs[d]
            r0 = (c if d == 0 else nch - 1 - c) * C
            cs = h * LANES
            vbs.append((u_ref[0, r0:r0 + C, cs:cs + LANES].astype(F32) - r[0:C]).astype(BF16))
        for (d, h), vb in zip(chains, vbs):
            u_ref, w_ref, kdt_ref, qd_ref, qk_ref, eg_ref, o_ref = refs[d]
            cc = c if d == 0 else nch - 1 - c
            lt, par = divmod(cc, 2)
            cs = h * LANES
            v_par = jnp.concatenate([vb, zc], axis=0) if par == 0 else jnp.concatenate([zc, vb], axis=0)
            kdt = kdt_ref[0, cs:cs + LANES, lt * LANES:(lt + 1) * LANES]
            eg = jnp.broadcast_to(eg_ref[0, cc, 0:1, d * H + h:d * H + h + 1], (LANES, LANES))
            s_scr[d, h] = s_scr[d, h] * eg + jnp.dot(kdt, v_par, preferred_element_type=F32)
        for d in range(2):
            u_ref, w_ref, kdt_ref, qd_ref, qk_ref, eg_ref, o_ref = refs[d]
            r0 = (c if d == 0 else nch - 1 - c) * C
            for p in range(H // 2):
                v0, v1 = vbs[d * H + 2 * p], vbs[d * H + 2 * p + 1]
                v_pair = jnp.concatenate([jnp.concatenate([v0, zc], axis=1), jnp.concatenate([zc, v1], axis=1)], axis=0)
                intra = jnp.dot(qk_ref[0, r0:r0 + C, p * LANES:(p + 1) * LANES], v_pair, preferred_element_type=F32)
                for hp in range(2):
                    h = 2 * p + hp
                    o = rs[d * H + h][C:2 * C] + intra[:, hp * LANES:(hp + 1) * LANES]
                    o_ref[0, r0:r0 + C, h * LANES:(h + 1) * LANES] = o.astype(BF16)

    @pl.when(i == n - 1)
    def _():
        sfin_ref[0] = s_scr[...]


def _chain_call(u, w, kdt, qd, qk, eg, s0, *, tc):
    b, l, _ = u.shape
    n = l // tc
    hw = GDN_HEADS * LANES
    fwd = lambda i, j: (i, j, 0)
    bwd = lambda i, j: (i, n - 1 - j, 1)
    bwd0 = lambda i, j: (i, n - 1 - j, 0)
    qkw = GDN_HEADS * CHUNK
    nchunk = tc // CHUNK
    in_specs = [
        pl.BlockSpec((1, tc, hw), fwd), pl.BlockSpec((1, tc, hw), fwd),
        pl.BlockSpec((1, hw, tc), lambda i, j: (i, 0, j)),
        pl.BlockSpec((1, tc, hw), fwd), pl.BlockSpec((1, tc, qkw), fwd),
        pl.BlockSpec((1, nchunk, 8, LANES), lambda i, j: (i, j, 0, 0)),
        pl.BlockSpec((1, tc, hw), bwd), pl.BlockSpec((1, tc, hw), bwd),
        pl.BlockSpec((1, hw, tc), lambda i, j: (i, 1, n - 1 - j)),
        pl.BlockSpec((1, tc, hw), bwd), pl.BlockSpec((1, tc, qkw), bwd),
        pl.BlockSpec((1, nchunk, 8, LANES), lambda i, j: (i, n - 1 - j, 0, 0)),
        pl.BlockSpec((1, 2, GDN_HEADS, LANES, LANES), lambda i, j: (i, 0, 0, 0, 0)),
    ]
    out_specs = [
        pl.BlockSpec((1, tc, hw), fwd),
        pl.BlockSpec((1, tc, hw), bwd0),
        pl.BlockSpec((1, 2, GDN_HEADS, LANES, LANES), lambda i, j: (i, 0, 0, 0, 0)),
    ]
    out_shape = [
        jax.ShapeDtypeStruct((b, l, hw), BF16),
        jax.ShapeDtypeStruct((b, l, hw), BF16),
        jax.ShapeDtypeStruct((b, 2, GDN_HEADS, LANES, LANES), F32),
    ]
    return pl.pallas_call(
        functools.partial(_chain_kernel, tc=tc),
        grid=(b, n),
        in_specs=in_specs,
        out_specs=out_specs,
        out_shape=out_shape,
        scratch_shapes=[pltpu.VMEM((2, GDN_HEADS, LANES, LANES), F32)],
        compiler_params=_cparams(("arbitrary", "arbitrary")),
        name="gdn_chain",
    )(u, w, kdt, qd, qk, eg, u, w, kdt, qd, qk, eg, s0)


def _attn_kernel(sink_ref, q_ref, kvp_ref, kvo_ref, kvn_ref, kvc_ref, zb_ref, o_ref, *, local, layer):
    nb = pl.num_programs(1)
    n = pl.program_id(1)
    T = ATT_BLOCK
    q = q_ref[0]
    qs = jnp.concatenate([q[:, g * LANES:(g + 1) * LANES] for g in range(ATT_GROUP)], axis=0)
    lane = lax.broadcasted_iota(jnp.int32, (1, LANES), 1)
    rowi = lax.broadcasted_iota(jnp.int32, (ATT_GROUP * T, T), 0) % T
    colj = lax.broadcasted_iota(jnp.int32, (ATT_GROUP * T, T), 1)
    grp = lax.broadcasted_iota(jnp.int32, (ATT_GROUP * T, 1), 0) // T

    pieces = [(kvc_ref, None)]
    if local:
        pieces += [(kvp_ref, jnp.logical_and(colj >= rowi, n > 0)),
                   (kvo_ref, None),
                   (kvn_ref, jnp.logical_and(colj <= rowi, n < nb - 1))]

    heads = range(ATT_KV_HEADS)
    vcats, tiles = [], []
    for kvh in heads:
        kmask = ((lane // ROPE_F) % ATT_KV_HEADS == kvh)
        vmask = (lane // ATT_HD == kvh)
        k_rows, v_rows = [], []
        for ref, _ in pieces:
            kv = ref[0]
            zero = jnp.zeros_like(kv[:, 0:LANES])
            k_rows.append(jnp.where(kmask, kv[:, 0:LANES], zero))
            v_rows.append(jnp.where(vmask, kv[:, LANES:2 * LANES], zero))
        vcats.append(jnp.concatenate(v_rows, axis=0))
        s_h = lax.dot_general(qs, jnp.concatenate(k_rows, axis=0), (((1,), (1,)), ((), ())),
                              preferred_element_type=F32)
        th, col = [], 0
        for ref, valid in pieces:
            for c in range(ref.shape[1] // LANES):
                t = s_h[:, col:col + LANES]
                th.append(t if valid is None else jnp.where(valid, t, NEG_INF))
                col += LANES
        tiles.append(th)
    sinks, ms = [], []
    for kvh in heads:
        sink = jnp.zeros((ATT_GROUP * T, 1), F32)
        for g in range(ATT_GROUP):
            sink = jnp.where(grp == g, sink_ref[layer, kvh * ATT_GROUP + g] * LOG2E, sink)
        sinks.append(sink)
        mt = tiles[kvh][0]
        for t in tiles[kvh][1:]:
            mt = jnp.maximum(mt, t)
        ms.append(jnp.maximum(sink, jnp.max(mt, axis=-1, keepdims=True)))
    pcats, inv_den = [], []
    for kvh in heads:
        pk = [jnp.exp2(t - ms[kvh]) for t in tiles[kvh]]
        st = pk[0]
        for t in pk[1:]:
            st = st + t
        inv_den.append(1.0 / (jnp.exp2(sinks[kvh] - ms[kvh]) + jnp.sum(st, axis=-1, keepdims=True)))
        pcats.append(jnp.concatenate([p.astype(BF16) for p in pk], axis=1))
    acc = jnp.zeros((ATT_GROUP * T, LANES), F32)
    for kvh in heads:
        acc = acc + jnp.dot(pcats[kvh], vcats[kvh], preferred_element_type=F32) * inv_den[kvh]
    for g in range(ATT_GROUP):
        og = acc[g * T:(g + 1) * T, :] * zb_ref[0, :, g * LANES:(g + 1) * LANES].astype(F32)
        o_ref[0, :, g * LANES:(g + 1) * LANES] = og.astype(BF16)


def _attn_call(sink, layer, q, kv, kvc, zb, *, local):
    b, l, _ = q.shape
    nb = l // ATT_BLOCK
    lc = kvc.shape[1]
    blk = lambda f: pl.BlockSpec((1, ATT_BLOCK, 2 * LANES), f)
    return pl.pallas_call(
        functools.partial(_attn_kernel, local=local, layer=layer),
        grid=(b, nb),
        in_specs=[
            pl.BlockSpec(memory_space=pltpu.SMEM),
            pl.BlockSpec((1, ATT_BLOCK, ATT_WIDTH), lambda i, j: (i, j, 0)),
            blk(lambda i, j: (i, jnp.maximum(j - 1, 0), 0)),
            blk(lambda i, j: (i, j, 0)),
            blk(lambda i, j: (i, jnp.minimum(j + 1, nb - 1), 0)),
            pl.BlockSpec((1, lc, 2 * LANES), lambda i, j: (i, 0, 0)),
            pl.BlockSpec((1, ATT_BLOCK, ATT_WIDTH), lambda i, j: (i, j, 0)),
        ],
        out_specs=pl.BlockSpec((1, ATT_BLOCK, ATT_WIDTH), lambda i, j: (i, j, 0)),
        out_shape=jax.ShapeDtypeStruct((b, l, ATT_WIDTH), BF16),
        compiler_params=_cparams(("arbitrary", "arbitrary")),
        name="attn_local" if local else "attn_ctx",
    )(sink, q, kv, kv, kv, kvc, zb)


def _merge_kernel(x_ref, gate_ref, of_ref, ob_ref, za_ref, yb_ref, ga_ref, gb_ref, gnw_ref,
                  wpa_ref, wpb_ref, wout_ref, o_ref):
    gnw = gnw_ref[...]
    ya_parts = []
    for h in range(GDN_HEADS):
        sl = slice(h * LANES, (h + 1) * LANES)
        o = of_ref[0, :, sl].astype(F32) + ob_ref[0, :, sl].astype(F32)
        on = o * lax.rsqrt(jnp.mean(o * o, axis=-1, keepdims=True) + EPS) * gnw
        ya_parts.append((on * za_ref[0, :, sl].astype(F32)).astype(BF16))
    ya = jnp.concatenate(ya_parts, axis=1)
    pa = jnp.dot(ya, wpa_ref[...], preferred_element_type=F32)
    pb = jnp.dot(yb_ref[0], wpb_ref[...], preferred_element_type=F32)
    y = ga_ref[0].astype(F32) * pa + gb_ref[0].astype(F32) * pb
    out = jnp.dot(y.astype(BF16), wout_ref[...], preferred_element_type=F32)
    o_ref[0] = x_ref[0] + gate_ref[0] * out


def _merge_call(x, mod, mod_row, layer, of, ob, za, yb, ga, gb, gnw, wpa, wpb, wout, *, tm):
    b, l, _ = x.shape
    rows = mod.shape[0] // (3 * DEPTH)
    row = lambda w: pl.BlockSpec((1, tm, w), lambda i, j: (i, j, 0))
    return pl.pallas_call(
        _merge_kernel,
        grid=(b, l // tm),
        in_specs=[
            row(D_MODEL), _mod_spec(layer, 2, rows, mod_row),
            row(GDN_WIDTH), row(GDN_WIDTH), row(GDN_WIDTH), row(ATT_WIDTH), row(D_MODEL), row(D_MODEL),
            _layer_spec(layer, 1, LANES),
            _layer_spec(layer, GDN_WIDTH, D_MODEL),
            _layer_spec(layer, ATT_WIDTH, D_MODEL),
            _layer_spec(layer, D_MODEL, D_MODEL),
        ],
        out_specs=row(D_MODEL),
        out_shape=jax.ShapeDtypeStruct((b, l, D_MODEL), F32),
        compiler_params=_cparams(("arbitrary", "arbitrary")),
        name="merge",
    )(x, mod, of, ob, za, yb, ga, gb, gnw, wpa, wpb, wout)


def _pick_tile(l, pref):
    t = min(pref, l)
    while l % t:
        t //= 2
    return t


def kernel(x, c, ctx, c_ctx, norm_w, w_mod, b_mod, w_in, conv_w, a_log, dt_bias, gdn_norm_w,
           q_norm_w, k_norm_w, sink, w_proj_a, w_proj_b, w_out):
    b, l, _ = x.shape
    lc = ctx.shape[1]
    assert l % ATT_BLOCK == 0 and lc % ATT_BLOCK == 0 and x.shape[2] == D_MODEL

    rows = ((b + 1 + 7) // 8) * 8
    c_all = jnp.concatenate([c, c_ctx[None, :], jnp.zeros((rows - b - 1, D_MODEL), F32)], axis=0)
    mod = _mod_call(c_all, w_mod, b_mod).reshape(DEPTH * 3 * rows, 1, D_MODEL)

    cos_l, sin_l = _rope_tables(l)
    cos_c = jnp.ones((lc, LANES), F32)
    sin_c = jnp.zeros((lc, LANES), F32)
    grp = (np.arange(LANES) // ROPE_F) % ATT_KV_HEADS
    gmat = jnp.asarray((grp[:, None] == grp[None, :]).astype(np.float32), dtype=BF16)
    s0 = jnp.zeros((b, 2, GDN_HEADS, LANES, LANES), F32)
    tm_l = _pick_tile(l, 512)
    tm_c = _pick_tile(lc, 256)
    tc_l = _pick_tile(l, 256)
    tc_c = _pick_tile(lc, 256)
    tch_l = _pick_tile(l, 512)

    w_packed_all = _pack_w_in(w_in)
    ng = 2 * GDN_HEADS
    gpad = jnp.zeros((DEPTH, ng), F32)
    gvec_all = jnp.concatenate([
        jnp.concatenate([gpad, jnp.exp(a_log.astype(F32)).reshape(DEPTH, ng),
                         jnp.zeros((DEPTH, LANES - 2 * ng), F32)], axis=1)[:, None, :],
        jnp.concatenate([gpad, dt_bias.astype(F32).reshape(DEPTH, ng),
                         jnp.zeros((DEPTH, LANES - 2 * ng), F32)], axis=1)[:, None, :],
        jnp.zeros((DEPTH, 6, LANES), F32)], axis=1)
    qn_all = (jax.vmap(_norm_lanes)(q_norm_w) * (ATT_HD ** -0.5 * LOG2E))[:, None, :]
    kn_all = jax.vmap(_norm_lanes)(k_norm_w)[:, None, :]
    cw_all = jnp.pad(conv_w, ((0, 0), (0, 8 - SHORT_CONV), (0, 0)))
    normw_all = norm_w[:, None, :]
    gnw_all = gdn_norm_w[:, None, :]
    wpa_all = w_proj_a.astype(BF16)
    wpb_all = jnp.swapaxes(w_proj_b.reshape(DEPTH, ATT_KV_HEADS, ATT_GROUP, ATT_HD, D_MODEL), 1, 2).reshape(
        DEPTH, ATT_WIDTH, D_MODEL).astype(BF16)
    wout_all = w_out.astype(BF16)
    proj_params = (normw_all, w_packed_all, gvec_all, qn_all, kn_all, gmat)
    merge_params = (gnw_all, wpa_all, wpb_all, wout_all)

    for i in range(DEPTH):
        update_ctx = i < DEPTH - 1

        outs_c = _proj_call(ctx, mod, b, i, *proj_params, cos_c, sin_c, with_rest=update_ctx, tm=tm_c)
        qkv_c, gates_c, kv_c = outs_c[:3]
        prep_c = _prep_call(qkv_c, cw_all, i, gates_c, tc=tc_c)
        of_c, ob_c, s_ctx = _chain_call(*prep_c, s0, tc=tc_c)

        qkv_l, gates_l, kv_l, za_l, q_l, zb_l, ga_l, gb_l = _proj_call(
            x, mod, None, i, *proj_params, cos_l, sin_l, with_rest=True, tm=tm_l)
        prep_l = _prep_call(qkv_l, cw_all, i, gates_l, tc=tc_l)
        of_l, ob_l, _ = _chain_call(*prep_l, s_ctx, tc=tch_l)
        yb_l = _attn_call(sink, i, q_l, kv_l, kv_c, zb_l, local=True)
        x_new = _merge_call(x, mod, None, i, of_l, ob_l, za_l, yb_l, ga_l, gb_l, *merge_params, tm=tm_l)

        if update_ctx:
            za_c, q_c, zb_c, ga_c, gb_c = outs_c[3:]
            yb_c = _attn_call(sink, i, q_c, kv_c, kv_c, zb_c, local=False)
            ctx = _merge_call(ctx, mod, b, i, of_c, ob_c, za_c, yb_c, ga_c, gb_c, *merge_params, tm=tm_c)
        x = x_new
    return x
```

```python
import functools
import math

import numpy as np
import jax
import jax.numpy as jnp
from jax import lax
from jax.experimental import pallas as pl
from jax.experimental.pallas import tpu as pltpu

F32 = jnp.float32
BF16 = jnp.bfloat16

D_MODEL = 1024
DEPTH = 2
GRID_W = 64
EPS = 1e-6
NEG_INF = -1e30

GDN_HEADS = 4
GDN_DK = 128
GDN_DV = 128
GDN_QKV = GDN_HEADS * (2 * GDN_DK + GDN_DV)
GDN_WIDTH = GDN_HEADS * GDN_DV
SHORT_CONV = 5
CHUNK = 64

ATT_HEADS = 8
ATT_KV_HEADS = 2
ATT_GROUP = ATT_HEADS // ATT_KV_HEADS
ATT_HD = 64
ATT_WIDTH = ATT_HEADS * ATT_HD
ATT_KV_WIDTH = ATT_KV_HEADS * ATT_HD
ATT_BLOCK = 128
ROPE_BASE = 10000.0
AXIS_DIM = ATT_HD // 2
ROPE_F = AXIS_DIM // 2
LOG2E = math.log2(math.e)

N_STATE = GDN_QKV + 4 * GDN_HEADS + 2 * ATT_KV_WIDTH

LANES = 128
P_QKV = 0
P_GATE = GDN_QKV
P_K = P_GATE + LANES
P_V = P_K + ATT_KV_WIDTH
P_STATE_END = P_V + ATT_KV_WIDTH
P_ZA = P_STATE_END
P_Q = P_ZA + GDN_WIDTH
P_ZB = P_Q + ATT_WIDTH
P_GA = P_ZB + ATT_WIDTH
P_GB = P_GA + D_MODEL
P_END = P_GB + D_MODEL

VMEM_LIMIT = 56 * 1024 * 1024


def _cparams(sem):
    return pltpu.CompilerParams(dimension_semantics=sem, vmem_limit_bytes=VMEM_LIMIT)


def _k_cols(w):
    lead = w.shape[:-1]
    w = w.reshape(lead + (ATT_KV_HEADS, 4, ROPE_F))
    return jnp.swapaxes(w, -3, -2).reshape(lead + (ATT_KV_WIDTH,))


def _q_cols(w):
    lead = w.shape[:-1]
    w = w.reshape(lead + (ATT_KV_HEADS, ATT_GROUP, 4, ROPE_F))
    return jnp.moveaxis(w, -4, -2).reshape(lead + (ATT_WIDTH,))


def _o_cols(w):
    lead = w.shape[:-1]
    w = w.reshape(lead + (ATT_KV_HEADS, ATT_GROUP, ATT_HD))
    return jnp.swapaxes(w, -3, -2).reshape(lead + (ATT_WIDTH,))


def _norm_lanes(w):
    w = w.reshape(4, 1, ROPE_F)
    return jnp.broadcast_to(w, (4, ATT_KV_HEADS, ROPE_F)).reshape(LANES)


def _pack_w_in(w_in):
    w_in = w_in.astype(BF16)
    o = 0
    qkv = w_in[..., o:o + GDN_QKV]; o += GDN_QKV
    gates = w_in[..., o:o + 4 * GDN_HEADS]; o += 4 * GDN_HEADS
    kb = w_in[..., o:o + ATT_KV_WIDTH]; o += ATT_KV_WIDTH
    vb = w_in[..., o:o + ATT_KV_WIDTH]; o += ATT_KV_WIDTH
    za = w_in[..., o:o + GDN_WIDTH]; o += GDN_WIDTH
    qb = w_in[..., o:o + ATT_WIDTH]; o += ATT_WIDTH
    zb = w_in[..., o:o + ATT_WIDTH]; o += ATT_WIDTH
    ga = w_in[..., o:o + D_MODEL]; o += D_MODEL
    gb = w_in[..., o:o + D_MODEL]
    gates = jnp.concatenate([gates, jnp.zeros(gates.shape[:-1] + (LANES - 4 * GDN_HEADS,), BF16)], axis=-1)
    return jnp.concatenate(
        [qkv, gates, _k_cols(kb), vb, za, _q_cols(qb), _o_cols(zb), ga, gb], axis=-1)


def _rope_tables(seq_len):
    t = jnp.arange(seq_len, dtype=F32)
    pos_row = jnp.floor(t / GRID_W)
    pos_col = t - pos_row * GRID_W
    inv = ROPE_BASE ** (-jnp.arange(0, AXIS_DIM, 2, dtype=F32) / AXIS_DIM)
    ang_r = pos_row[:, None] * inv[None, :]
    ang_c = pos_col[:, None] * inv[None, :]

    def lay(a):
        return jnp.concatenate([a, a], axis=1)

    cos = jnp.concatenate([lay(jnp.cos(ang_r)), lay(jnp.cos(ang_r)), lay(jnp.cos(ang_c)), lay(jnp.cos(ang_c))], axis=1)
    sin = jnp.concatenate([-lay(jnp.sin(ang_r)), lay(jnp.sin(ang_r)), -lay(jnp.sin(ang_c)), lay(jnp.sin(ang_c))], axis=1)
    return cos, sin


def _mod_kernel(c_ref, w_ref, b_ref, o_ref):
    c = c_ref[...]
    s = c * jax.nn.sigmoid(c)
    o_ref[0, 0] = jnp.dot(s.astype(BF16), w_ref[0].astype(BF16), preferred_element_type=F32) + b_ref[0]


def _mod_call(c_all, w_mod, b_mod):
    rows = c_all.shape[0]
    nblk = 3
    return pl.pallas_call(
        _mod_kernel,
        grid=(DEPTH, nblk),
        in_specs=[
            pl.BlockSpec((rows, D_MODEL), lambda l, j: (0, 0)),
            pl.BlockSpec((1, D_MODEL, D_MODEL), lambda l, j: (l, 0, j)),
            pl.BlockSpec((1, 1, D_MODEL), lambda l, j: (l, 0, j)),
        ],
        out_specs=pl.BlockSpec((1, 1, rows, D_MODEL), lambda l, j: (l, j, 0, 0)),
        out_shape=jax.ShapeDtypeStruct((DEPTH, nblk, rows, D_MODEL), F32),
        compiler_params=_cparams(("arbitrary", "arbitrary")),
        name="mod",
    )(c_all, w_mod, b_mod.reshape(DEPTH, 1, 3 * D_MODEL))


def _mod_spec(layer, kind, rows, row):
    base = (layer * 3 + kind) * rows
    if row is None:
        return pl.BlockSpec((1, 1, D_MODEL), lambda i, j: (base + i, 0, 0))
    return pl.BlockSpec((1, 1, D_MODEL), lambda i, j: (base + row, 0, 0))


def _layer_spec(layer, *block):
    zeros = (0,) * len(block)
    return pl.BlockSpec((None,) + tuple(block), lambda i, j: (layer,) + zeros)


def _swap32(t):
    lane = lax.broadcasted_iota(jnp.int32, t.shape, 1)
    even = (lane // 32) % 2 == 0
    return jnp.where(even, pltpu.roll(t, 96, 1), pltpu.roll(t, 32, 1))


def _headnorm_rope(parts, nws, gmat, cos, sin):
    t = jnp.concatenate(parts, axis=0) if len(parts) > 1 else parts[0]
    ss = jnp.dot((t * t).astype(BF16), gmat, preferred_element_type=F32)
    inv = lax.rsqrt(ss * (1.0 / ATT_HD) + EPS)
    tm = parts[0].shape[0]
    outs = []
    for i, nw in enumerate(nws):
        tn = parts[i] * inv[i * tm:(i + 1) * tm] * nw
        outs.append(tn * cos + _swap32(tn) * sin)
    return outs


def _softplus(x):
    return jnp.maximum(x, 0.0) + jnp.log1p(jnp.exp(-jnp.abs(x)))


def _proj_kernel(x_ref, scale_ref, shift_ref, normw_ref, w_ref, gvec_ref, qn_ref, kn_ref, gmat_ref,
                 cos_ref, sin_ref, *out_refs, with_rest):
    if with_rest:
        qkv_ref, gates_ref, kv_ref, za_ref, q_ref, zb_ref, ga_ref, gb_ref = out_refs
    else:
        qkv_ref, gates_ref, kv_ref = out_refs
    x = x_ref[0]
    ms = jnp.mean(x * x, axis=-1, keepdims=True)
    a = normw_ref[...] * (1.0 + scale_ref[0])
    h = (x * lax.rsqrt(ms + EPS) * a + shift_ref[0]).astype(BF16)

    def proj(lo, width):
        return jnp.dot(h, w_ref[:, lo:lo + width], preferred_element_type=F32)

    for c in range(GDN_QKV // 512):
        qkv_ref[0, :, c * 512:(c + 1) * 512] = proj(P_QKV + c * 512, 512).astype(BF16)

    g = proj(P_GATE, LANES)
    lane = lax.broadcasted_iota(jnp.int32, g.shape, 1)
    beta = jax.nn.sigmoid(g)
    dec = -gvec_ref[0:1, :] * _softplus(g + gvec_ref[1:2, :])
    gates_ref[0] = jnp.where(lane < 2 * GDN_HEADS, beta, dec)

    gmat = gmat_ref[...]
    cos = cos_ref[...]
    sin = sin_ref[...]
    kv_ref[0, :, LANES:2 * LANES] = proj(P_V, LANES).astype(BF16)
    parts = [proj(P_K, LANES)]
    nws = [kn_ref[...]]
    if with_rest:
        qraw = proj(P_Q, ATT_WIDTH)
        parts += [qraw[:, gi * LANES:(gi + 1) * LANES] for gi in range(ATT_GROUP)]
        nws += [qn_ref[...]] * ATT_GROUP
    normed = _headnorm_rope(parts, nws, gmat, cos, sin)
    kv_ref[0, :, 0:LANES] = normed[0].astype(BF16)

    if with_rest:
        for gi in range(ATT_GROUP):
            q_ref[0, :, gi * LANES:(gi + 1) * LANES] = normed[1 + gi].astype(BF16)
        z = proj(P_ZA, GDN_WIDTH)
        za_ref[0] = (z * jax.nn.sigmoid(z)).astype(BF16)
        z = proj(P_ZB, ATT_WIDTH)
        zb_ref[0] = (z * jax.nn.sigmoid(z)).astype(BF16)
        for c in range(D_MODEL // 512):
            ga_ref[0, :, c * 512:(c + 1) * 512] = jax.nn.sigmoid(proj(P_GA + c * 512, 512)).astype(BF16)
            gb_ref[0, :, c * 512:(c + 1) * 512] = jax.nn.sigmoid(proj(P_GB + c * 512, 512)).astype(BF16)


def _proj_call(x, mod, mod_row, layer, normw, w_packed, gvec, qn, kn, gmat, cos, sin, *, with_rest, tm):
    b, l, _ = x.shape
    nw = P_END if with_rest else P_STATE_END
    rows = mod.shape[0] // (3 * DEPTH)
    const2 = lambda i, j: (0, 0)
    row_map = lambda i, j: (i, j, 0)
    in_specs = [
        pl.BlockSpec((1, tm, D_MODEL), row_map),
        _mod_spec(layer, 1, rows, mod_row),
        _mod_spec(layer, 0, rows, mod_row),
        _layer_spec(layer, 1, D_MODEL),
        _layer_spec(layer, D_MODEL, nw),
        _layer_spec(layer, 8, LANES),
        _layer_spec(layer, 1, LANES),
        _layer_spec(layer, 1, LANES),
        pl.BlockSpec((LANES, LANES), const2),
        pl.BlockSpec((tm, LANES), lambda i, j: (j, 0)),
        pl.BlockSpec((tm, LANES), lambda i, j: (j, 0)),
    ]
    widths = [(GDN_QKV, BF16), (LANES, F32), (2 * LANES, BF16)]
    if with_rest:
        widths += [(GDN_WIDTH, BF16), (ATT_WIDTH, BF16), (ATT_WIDTH, BF16), (D_MODEL, BF16), (D_MODEL, BF16)]
    out_specs = [pl.BlockSpec((1, tm, w), row_map) for w, _ in widths]
    out_shape = [jax.ShapeDtypeStruct((b, l, w), dt) for w, dt in widths]
    return pl.pallas_call(
        functools.partial(_proj_kernel, with_rest=with_rest),
        grid=(b, l // tm),
        in_specs=in_specs,
        out_specs=out_specs,
        out_shape=out_shape,
        compiler_params=_cparams(("arbitrary", "arbitrary")),
        name="proj_full" if with_rest else "proj_state",
    )(x, mod, mod, normw, w_packed, gvec, qn, kn, gmat, cos, sin)


HALO = 16


def _blockdiag(y, isf):
    zero = jnp.zeros_like(y)
    return jnp.concatenate([jnp.where(isf, y, zero), jnp.where(isf, zero, y)], axis=0)


def _short_conv_stage(raw_ref, prev_ref, next_ref, cw_ref, raw_scr, qkv_scr, tc):
    j = pl.program_id(1)
    n = pl.num_programs(1)
    half = SHORT_CONV // 2
    for cb in range(GDN_QKV // LANES):
        sl = slice(cb * LANES, (cb + 1) * LANES)
        raw_scr[cb, 0:HALO, :] = jnp.where(j > 0, prev_ref[0, :, sl].astype(F32), 0.0)
        raw_scr[cb, HALO:HALO + tc, :] = raw_ref[0, :, sl].astype(F32)
        raw_scr[cb, HALO + tc:2 * HALO + tc, :] = jnp.where(j < n - 1, next_ref[0, :, sl].astype(F32), 0.0)
        y = raw_scr[cb, HALO - half:HALO - half + tc, :] * cw_ref[0:1, sl]
        for tap in range(1, SHORT_CONV):
            y = y + raw_scr[cb, HALO - half + tap:HALO - half + tap + tc, :] * cw_ref[tap:tap + 1, sl]
        y = y * jax.nn.sigmoid(y)
        if cb < 2 * GDN_HEADS:
            scale = GDN_DK ** -0.5 if cb < GDN_HEADS else 1.0
            y = y * (lax.rsqrt(jnp.sum(y * y, axis=-1, keepdims=True) + EPS) * scale)
        qkv_scr[:, sl] = y.astype(BF16)


def _prep_kernel(raw_ref, prev_ref, next_ref, cw_ref, gates_ref, u_ref, w_ref, kdt_ref, qd_ref, qk_ref, eg_ref,
                 raw_scr, qkv_ref, a_scr, t_scr, rhs_scr, *, tc):
    H = GDN_HEADS
    C = CHUNK

    @pl.when(jnp.logical_and(pl.program_id(0) == 0, pl.program_id(1) == 0))
    def _():
        rhs_scr[...] = jnp.zeros(rhs_scr.shape, BF16)

    _short_conv_stage(raw_ref, prev_ref, next_ref, cw_ref, raw_scr, qkv_ref, tc)
    G = gates_ref[0]
    gT = G.T[2 * H:4 * H, :]
    lane_t = lax.broadcasted_iota(jnp.int32, gT.shape, 1)
    pos = lane_t % C
    pre = gT
    suf = gT
    for s in (1, 2, 4, 8, 16, 32):
        pre = pre + jnp.where(pos >= s, pltpu.roll(pre, s, 1), 0.0)
        suf = suf + jnp.where(pos < C - s, pltpu.roll(suf, tc - s, 1), 0.0)
    row8 = lax.broadcasted_iota(jnp.int32, gT.shape, 0)
    gcT = jnp.where(row8 < H, pre, suf)
    gc = jnp.concatenate([gcT, jnp.zeros((LANES - 2 * H, tc), F32)], axis=0).T

    lane = lax.broadcasted_iota(jnp.int32, (C, LANES), 1)
    ii = lax.broadcasted_iota(jnp.int32, (C, LANES), 0)
    jj = lane % C
    isf = lane < C
    isb = jnp.logical_not(isf)
    incl = jnp.logical_or(jnp.logical_and(isf, ii >= jj), jnp.logical_and(isb, ii <= jj))
    strict = jnp.logical_or(jnp.logical_and(isf, ii > jj), jnp.logical_and(isb, ii < jj))
    eye2 = (ii == jj).astype(F32)
    same = {kk: (ii // kk) == (jj // kk) for kk in (2, 4, 8, 16, 32, 64)}
    lane8 = lax.broadcasted_iota(jnp.int32, (1, LANES), 1)

    def colb(arr, r0, c):
        return jnp.broadcast_to(arr[r0:r0 + C, c:c + 1], (C, LANES))

    for s in range(tc // C):
        r0 = s * C
        glrow = jnp.where(lane8 < H, gc[r0 + C - 1:r0 + C, :], gc[r0:r0 + 1, :])
        eg_ref[0, s] = jnp.broadcast_to(jnp.exp(glrow), (8, LANES))

    items = [(lt, 2 * p + hp, halfsel) for lt in range(tc // LANES) for p in range(H // 2)
             for halfsel in range(2) for hp in range(2)]
    gts = {}
    for lt in range(tc // LANES):
        gt = gcT[:, lt * LANES:(lt + 1) * LANES]
        gts[lt] = (gt, pltpu.roll(gt, C, 1))

    kd_parts = {}
    qk_hold = {}

    def stage1(idx):
        lt, h, halfsel = items[idx]
        gt, gt_r = gts[lt]
        r0 = (2 * lt + halfsel) * C
        if halfsel == 0:
            row_f, row_b = gt[h:h + 1, :], gt_r[H + h:H + h + 1, :]
        else:
            row_f, row_b = gt_r[h:h + 1, :], gt[H + h:H + h + 1, :]
        gc_row2 = jnp.broadcast_to(jnp.where(lane8 < C, row_f, row_b), (C, LANES))
        gcf = colb(gc, r0, h)
        gcb = colb(gc, r0, H + h)
        dec = jnp.where(incl, jnp.exp(jnp.where(incl, jnp.where(isf, gcf, gcb) - gc_row2, 0.0)), 0.0)
        bf = colb(G, r0, h)
        bb = colb(G, r0, H + h)
        qt = qkv_ref[r0:r0 + C, h * LANES:(h + 1) * LANES]
        kt = qkv_ref[r0:r0 + C, (H + h) * LANES:(H + h + 1) * LANES]
        vt = qkv_ref[r0:r0 + C, (2 * H + h) * LANES:(2 * H + h + 1) * LANES]
        kq = lax.dot_general(jnp.concatenate([kt, qt], axis=0), jnp.concatenate([kt, kt], axis=0),
                             (((1,), (1,)), ((), ())), preferred_element_type=F32)
        a2 = jnp.where(strict, jnp.where(isf, bf, bb) * kq[0:C] * dec, 0.0)
        a_scr[idx] = a2
        t_scr[idx] = eye2 - jnp.where(same[2], a2, 0.0)
        qk2 = kq[C:2 * C] * dec
        if h % 2 == 0:
            qk_hold[(lt, h, halfsel)] = qk2
        else:
            qk_even = qk_hold.pop((lt, h - 1, halfsel))
            pc = (h // 2) * LANES
            qk_ref[0, 0, r0:r0 + C, pc:pc + LANES] = jnp.where(isf, qk_even, pltpu.roll(qk2, C, 1)).astype(BF16)
            qk_ref[0, 1, r0:r0 + C, pc:pc + LANES] = jnp.where(isf, pltpu.roll(qk_even, C, 1), qk2).astype(BF16)

        kf = kt.astype(F32)
        vf = vt.astype(F32)
        qf = qt.astype(F32)
        egf = jnp.exp(gcf)
        egb = jnp.exp(gcb)
        rhs_scr[idx, 0:C, 0:2 * LANES] = jnp.concatenate([vf * bf, kf * (bf * egf)], axis=1).astype(BF16)
        rhs_scr[idx, C:2 * C, 2 * LANES:4 * LANES] = jnp.concatenate([vf * bb, kf * (bb * egb)], axis=1).astype(BF16)
        cf = h * LANES
        qd_ref[0, 0, r0:r0 + C, cf:cf + LANES] = (qf * egf).astype(BF16)
        qd_ref[0, 1, r0:r0 + C, cf:cf + LANES] = (qf * egb).astype(BF16)
        glf = jnp.broadcast_to(gc[r0 + C - 1:r0 + C, h:h + 1], (C, LANES))
        glb = jnp.broadcast_to(gc[r0:r0 + 1, H + h:H + h + 1], (C, LANES))
        kd_parts[(lt, h, halfsel)] = (kf * jnp.exp(glf - gcf), kf * jnp.exp(glb - gcb))
        if halfsel == 1:
            for d in range(2):
                kdt = jnp.concatenate([kd_parts[(lt, h, 0)][d], kd_parts[(lt, h, 1)][d]], axis=0).T
                kdt_ref[0, 0, (d * H + h) * LANES:(d * H + h + 1) * LANES, lt * LANES:(lt + 1) * LANES] = kdt.astype(BF16)

    levels = (2, 4, 8, 16, 32)

    def round_a(kk, idxs):
        emask = jnp.logical_and(same[2 * kk], jnp.logical_not(same[kk]))
        ps = []
        for idx in idxs:
            e = jnp.where(emask, a_scr[idx], 0.0).astype(BF16)
            ps.append(jnp.dot(e, _blockdiag(t_scr[idx].astype(BF16), isf), preferred_element_type=F32))
        return ps

    def round_b(idxs, ps):
        for idx, p in zip(idxs, ps):
            t2 = t_scr[idx]
            t_scr[idx] = t2 - jnp.dot(t2.astype(BF16), _blockdiag(p.astype(BF16), isf),
                                      preferred_element_type=F32)

    def stage3(idx):
        lt, h, halfsel = items[idx]
        r0 = (2 * lt + halfsel) * C
        uw = jnp.dot(t_scr[idx].astype(BF16), rhs_scr[idx], preferred_element_type=F32)
        cf = h * LANES
        u_ref[0, 0, r0:r0 + C, cf:cf + LANES] = uw[:, 0:LANES].astype(BF16)
        w_ref[0, 0, r0:r0 + C, cf:cf + LANES] = uw[:, LANES:2 * LANES].astype(BF16)
        u_ref[0, 1, r0:r0 + C, cf:cf + LANES] = uw[:, 2 * LANES:3 * LANES].astype(BF16)
        w_ref[0, 1, r0:r0 + C, cf:cf + LANES] = uw[:, 3 * LANES:4 * LANES].astype(BF16)

    everything = list(range(len(items)))
    for idx in everything:
        stage1(idx)
    for kk in levels:
        round_b(everything, round_a(kk, everything))
    for idx in everything:
        stage3(idx)


def _prep_call(qkv_raw, conv_w, layer, gates, *, tc):
    b, l, _ = qkv_raw.shape
    row_map = lambda i, j: (i, j, 0)
    wide = 2 * GDN_HEADS * LANES
    hw = GDN_HEADS * LANES
    dir_map = lambda i, j: (i, 0, j, 0)
    n_items = (tc // CHUNK) * GDN_HEADS
    hpt = tc // HALO
    nhalo = l // HALO
    n = l // tc
    return pl.pallas_call(
        functools.partial(_prep_kernel, tc=tc),
        grid=(b, l // tc),
        in_specs=[
            pl.BlockSpec((1, tc, GDN_QKV), row_map),
            pl.BlockSpec((1, HALO, GDN_QKV), lambda i, j: (i, jnp.maximum(j * hpt - 1, 0), 0)),
            pl.BlockSpec((1, HALO, GDN_QKV), lambda i, j: (i, jnp.minimum((j + 1) * hpt, nhalo - 1), 0)),
            _layer_spec(layer, 8, GDN_QKV),
            pl.BlockSpec((1, tc, LANES), row_map),
        ],
        out_specs=[
            pl.BlockSpec((1, 2, tc, hw), dir_map),
            pl.BlockSpec((1, 2, tc, hw), dir_map),
            pl.BlockSpec((1, 1, wide, tc), lambda i, j: (i, j, 0, 0)),
            pl.BlockSpec((1, 2, tc, hw), dir_map),
            pl.BlockSpec((1, 2, tc, GDN_HEADS * CHUNK), dir_map),
            pl.BlockSpec((1, tc // CHUNK, 8, LANES), lambda i, j: (i, j, 0, 0)),
        ],
        out_shape=[
            jax.ShapeDtypeStruct((b, 2, l, hw), BF16),
            jax.ShapeDtypeStruct((b, 2, l, hw), BF16),
            jax.ShapeDtypeStruct((b, l // tc, wide, tc), BF16),
            jax.ShapeDtypeStruct((b, 2, l, hw), BF16),
            jax.ShapeDtypeStruct((b, 2, l, GDN_HEADS * CHUNK), BF16),
            jax.ShapeDtypeStruct((b, l // CHUNK, 8, LANES), F32),
        ],
        scratch_shapes=[
            pltpu.VMEM((GDN_QKV // LANES, tc + 2 * HALO, LANES), F32),
            pltpu.VMEM((tc, GDN_QKV), BF16),
            pltpu.VMEM((n_items, CHUNK, LANES), F32),
            pltpu.VMEM((n_items, CHUNK, LANES), F32),
            pltpu.VMEM((n_items, 2 * CHUNK, 4 * LANES), BF16),
        ],
        compiler_params=_cparams(("arbitrary", "arbitrary")),
        name="gdn_prep",
    )(qkv_raw, qkv_raw, qkv_raw, conv_w, gates)


def _chain_kernel(uf_ref, wf_ref, kdtf_ref, qdf_ref, qkf_ref, egf_ref,
                  ub_ref, wb_ref, kdtb_ref, qdb_ref, qkb_ref, egb_ref, s0_ref,
                  of_ref, ob_ref, sfin_ref, s_scr, *, tc, tcp):
    H = GDN_HEADS
    C = CHUNK
    i = pl.program_id(1)
    n = pl.num_programs(1)

    @pl.when(i == 0)
    def _():
        s_scr[...] = s0_ref[0]

    nch = tc // C
    zc = jnp.zeros((C, LANES), BF16)
    refs = ((uf_ref, wf_ref, kdtf_ref, qdf_ref, qkf_ref, egf_ref, of_ref),
            (ub_ref, wb_ref, kdtb_ref, qdb_ref, qkb_ref, egb_ref, ob_ref))
    chains = [(d, h) for d in range(2) for h in range(H)]
    for c in range(nch):
        rs = []
        for d, h in chains:
            u_ref, w_ref, kdt_ref, qd_ref, qk_ref, eg_ref, o_ref = refs[d]
            r0 = (c if d == 0 else nch - 1 - c) * C
            cs = h * LANES
            lhs = jnp.concatenate([w_ref[0, 0, r0:r0 + C, cs:cs + LANES], qd_ref[0, 0, r0:r0 + C, cs:cs + LANES]],
                                  axis=0)
            rs.append(jnp.dot(lhs, s_scr[d, h].astype(BF16), preferred_element_type=F32))
        vbs = []
        for (d, h), r in zip(chains, rs):
            u_ref, w_ref, kdt_ref, qd_ref, qk_ref, eg_ref, o_ref = refs[d]
            r0 = (c if d == 0 else nch - 1 - c) * C
            cs = h * LANES
            vbs.append((u_ref[0, 0, r0:r0 + C, cs:cs + LANES].astype(F32) - r[0:C]).astype(BF16))
        for (d, h), vb in zip(chains, vbs):
            u_ref, w_ref, kdt_ref, qd_ref, qk_ref, eg_ref, o_ref = refs[d]
            cc = c if d == 0 else nch - 1 - c
            pt, within = divmod(cc * C, tcp)
            lt, par = divmod(within // C, 2)
            cs = h * LANES
            v_par = jnp.concatenate([vb, zc], axis=0) if par == 0 else jnp.concatenate([zc, vb], axis=0)
            kdt = kdt_ref[0, pt, cs:cs + LANES, lt * LANES:(lt + 1) * LANES]
            eg = jnp.broadcast_to(eg_ref[0, cc, 0:1, d * H + h:d * H + h + 1], (LANES, LANES))
            s_scr[d, h] = s_scr[d, h] * eg + jnp.dot(kdt, v_par, preferred_element_type=F32)
        for d in range(2):
            u_ref, w_ref, kdt_ref, qd_ref, qk_ref, eg_ref, o_ref = refs[d]
            r0 = (c if d == 0 else nch - 1 - c) * C
            for p in range(H // 2):
                v0, v1 = vbs[d * H + 2 * p], vbs[d * H + 2 * p + 1]
                v_pair = jnp.concatenate([jnp.concatenate([v0, zc], axis=1), jnp.concatenate([zc, v1], axis=1)], axis=0)
                intra = jnp.dot(qk_ref[0, 0, r0:r0 + C, p * LANES:(p + 1) * LANES], v_pair,
                                preferred_element_type=F32)
                for hp in range(2):
                    h = 2 * p + hp
                    o = rs[d * H + h][C:2 * C] + intra[:, hp * LANES:(hp + 1) * LANES]
                    o_ref[0, r0:r0 + C, h * LANES:(h + 1) * LANES] = o.astype(BF16)

    @pl.when(i == n - 1)
    def _():
        sfin_ref[0] = s_scr[...]


def _chain_call(u, w, kdt, qd, qk, eg, s0, *, tc):
    b, _, l, hw = u.shape
    tcp = kdt.shape[3]
    assert tc % tcp == 0
    tpr = tc // tcp
    n = l // tc
    fwd = lambda i, j: (i, 0, j, 0)
    bwd = lambda i, j: (i, 1, n - 1 - j, 0)
    fwd3 = lambda i, j: (i, j, 0)
    bwd3 = lambda i, j: (i, n - 1 - j, 0)
    qkw = GDN_HEADS * CHUNK
    nchunk = tc // CHUNK
    in_specs = [
        pl.BlockSpec((1, 1, tc, hw), fwd), pl.BlockSpec((1, 1, tc, hw), fwd),
        pl.BlockSpec((1, tpr, hw, tcp), lambda i, j: (i, j, 0, 0)),
        pl.BlockSpec((1, 1, tc, hw), fwd), pl.BlockSpec((1, 1, tc, qkw), fwd),
        pl.BlockSpec((1, nchunk, 8, LANES), lambda i, j: (i, j, 0, 0)),
        pl.BlockSpec((1, 1, tc, hw), bwd), pl.BlockSpec((1, 1, tc, hw), bwd),
        pl.BlockSpec((1, tpr, hw, tcp), lambda i, j: (i, n - 1 - j, 1, 0)),
        pl.BlockSpec((1, 1, tc, hw), bwd), pl.BlockSpec((1, 1, tc, qkw), bwd),
        pl.BlockSpec((1, nchunk, 8, LANES), lambda i, j: (i, n - 1 - j, 0, 0)),
        pl.BlockSpec((1, 2, GDN_HEADS, LANES, LANES), lambda i, j: (i, 0, 0, 0, 0)),
    ]
    out_specs = [
        pl.BlockSpec((1, tc, hw), fwd3),
        pl.BlockSpec((1, tc, hw), bwd3),
        pl.BlockSpec((1, 2, GDN_HEADS, LANES, LANES), lambda i, j: (i, 0, 0, 0, 0)),
    ]
    out_shape = [
        jax.ShapeDtypeStruct((b, l, hw), BF16),
        jax.ShapeDtypeStruct((b, l, hw), BF16),
        jax.ShapeDtypeStruct((b, 2, GDN_HEADS, LANES, LANES), F32),
    ]
    return pl.pallas_call(
        functools.partial(_chain_kernel, tc=tc, tcp=tcp),
        grid=(b, n),
        in_specs=in_specs,
        out_specs=out_specs,
        out_shape=out_shape,
        scratch_shapes=[pltpu.VMEM((2, GDN_HEADS, LANES, LANES), F32)],
        compiler_params=_cparams(("arbitrary", "arbitrary")),
        name="gdn_chain",
    )(u, w, kdt, qd, qk, eg, u, w, kdt, qd, qk, eg, s0)


def _attn_kernel(sink_ref, q_ref, kvp_ref, kvo_ref, kvn_ref, kvc_ref, zb_ref, o_ref, *, local, layer):
    nb = pl.num_programs(1)
    n = pl.program_id(1)
    T = ATT_BLOCK
    q = q_ref[0]
    qs = jnp.concatenate([q[:, g * LANES:(g + 1) * LANES] for g in range(ATT_GROUP)], axis=0)
    lane = lax.broadcasted_iota(jnp.int32, (1, LANES), 1)
    rowi = lax.broadcasted_iota(jnp.int32, (ATT_GROUP * T, T), 0) % T
    colj = lax.broadcasted_iota(jnp.int32, (ATT_GROUP * T, T), 1)
    grp = lax.broadcasted_iota(jnp.int32, (ATT_GROUP * T, 1), 0) // T

    pieces = [(kvc_ref, None)]
    if local:
        pieces += [(kvp_ref, jnp.logical_and(colj >= rowi, n > 0)),
                   (kvo_ref, None),
                   (kvn_ref, jnp.logical_and(colj <= rowi, n < nb - 1))]

    heads = range(ATT_KV_HEADS)
    vcats, tiles = [], []
    for kvh in heads:
        kmask = ((lane // ROPE_F) % ATT_KV_HEADS == kvh)
        vmask = (lane // ATT_HD == kvh)
        k_rows, v_rows = [], []
        for ref, _ in pieces:
            kv = ref[0]
            zero = jnp.zeros_like(kv[:, 0:LANES])
            k_rows.append(jnp.where(kmask, kv[:, 0:LANES], zero))
            v_rows.append(jnp.where(vmask, kv[:, LANES:2 * LANES], zero))
        vcats.append(jnp.concatenate(v_rows, axis=0))
        s_h = lax.dot_general(qs, jnp.concatenate(k_rows, axis=0), (((1,), (1,)), ((), ())),
                              preferred_element_type=F32)
        th, col = [], 0
        for ref, valid in pieces:
            for c in range(ref.shape[1] // LANES):
                t = s_h[:, col:col + LANES]
                th.append(t if valid is None else jnp.where(valid, t, NEG_INF))
                col += LANES
        tiles.append(th)
    sinks, ms = [], []
    for kvh in heads:
        sink = jnp.zeros((ATT_GROUP * T, 1), F32)
        for g in range(ATT_GROUP):
            sink = jnp.where(grp == g, sink_ref[layer, kvh * ATT_GROUP + g] * LOG2E, sink)
        sinks.append(sink)
        mt = tiles[kvh][0]
        for t in tiles[kvh][1:]:
            mt = jnp.maximum(mt, t)
        ms.append(jnp.maximum(sink, jnp.max(mt, axis=-1, keepdims=True)))
    pcats, inv_den = [], []
    for kvh in heads:
        pk = [jnp.exp2(t - ms[kvh]) for t in tiles[kvh]]
        st = pk[0]
        for t in pk[1:]:
            st = st + t
        inv_den.append(1.0 / (jnp.exp2(sinks[kvh] - ms[kvh]) + jnp.sum(st, axis=-1, keepdims=True)))
        pcats.append(jnp.concatenate([p.astype(BF16) for p in pk], axis=1))
    acc = jnp.zeros((ATT_GROUP * T, LANES), F32)
    for kvh in heads:
        acc = acc + jnp.dot(pcats[kvh], vcats[kvh], preferred_element_type=F32) * inv_den[kvh]
    for g in range(ATT_GROUP):
        og = acc[g * T:(g + 1) * T, :] * zb_ref[0, :, g * LANES:(g + 1) * LANES].astype(F32)
        o_ref[0, :, g * LANES:(g + 1) * LANES] = og.astype(BF16)


def _attn_call(sink, layer, q, kv, kvc, zb, *, local):
    b, l, _ = q.shape
    nb = l // ATT_BLOCK
    lc = kvc.shape[1]
    blk = lambda f: pl.BlockSpec((1, ATT_BLOCK, 2 * LANES), f)
    return pl.pallas_call(
        functools.partial(_attn_kernel, local=local, layer=layer),
        grid=(b, nb),
        in_specs=[
            pl.BlockSpec(memory_space=pltpu.SMEM),
            pl.BlockSpec((1, ATT_BLOCK, ATT_WIDTH), lambda i, j: (i, j, 0)),
            blk(lambda i, j: (i, jnp.maximum(j - 1, 0), 0)),
            blk(lambda i, j: (i, j, 0)),
            blk(lambda i, j: (i, jnp.minimum(j + 1, nb - 1), 0)),
            pl.BlockSpec((1, lc, 2 * LANES), lambda i, j: (i, 0, 0)),
            pl.BlockSpec((1, ATT_BLOCK, ATT_WIDTH), lambda i, j: (i, j, 0)),
        ],
        out_specs=pl.BlockSpec((1, ATT_BLOCK, ATT_WIDTH), lambda i, j: (i, j, 0)),
        out_shape=jax.ShapeDtypeStruct((b, l, ATT_WIDTH), BF16),
        compiler_params=_cparams(("arbitrary", "arbitrary")),
        name="attn_local" if local else "attn_ctx",
    )(sink, q, kv, kv, kv, kvc, zb)


def _merge_kernel(x_ref, gate_ref, of_ref, ob_ref, za_ref, yb_ref, ga_ref, gb_ref, gnw_ref,
                  wpa_ref, wpb_ref, wout_ref, o_ref):
    gnw = gnw_ref[...]
    ya_parts = []
    for h in range(GDN_HEADS):
        sl = slice(h * LANES, (h + 1) * LANES)
        o = of_ref[0, :, sl].astype(F32) + ob_ref[0, :, sl].astype(F32)
        on = o * lax.rsqrt(jnp.mean(o * o, axis=-1, keepdims=True) + EPS) * gnw
        ya_parts.append((on * za_ref[0, :, sl].astype(F32)).astype(BF16))
    ya = jnp.concatenate(ya_parts, axis=1)
    pa = jnp.dot(ya, wpa_ref[...], preferred_element_type=F32)
    pb = jnp.dot(yb_ref[0], wpb_ref[...], preferred_element_type=F32)
    y = ga_ref[0].astype(F32) * pa + gb_ref[0].astype(F32) * pb
    out = jnp.dot(y.astype(BF16), wout_ref[...], preferred_element_type=F32)
    o_ref[0] = x_ref[0] + gate_ref[0] * out


def _merge_call(x, mod, mod_row, layer, of, ob, za, yb, ga, gb, gnw, wpa, wpb, wout, *, tm):
    b, l, _ = x.shape
    rows = mod.shape[0] // (3 * DEPTH)
    row = lambda w: pl.BlockSpec((1, tm, w), lambda i, j: (i, j, 0))
    return pl.pallas_call(
        _merge_kernel,
        grid=(b, l // tm),
        in_specs=[
            row(D_MODEL), _mod_spec(layer, 2, rows, mod_row),
            row(GDN_WIDTH), row(GDN_WIDTH), row(GDN_WIDTH), row(ATT_WIDTH), row(D_MODEL), row(D_MODEL),
            _layer_spec(layer, 1, LANES),
            _layer_spec(layer, GDN_WIDTH, D_MODEL),
            _layer_spec(layer, ATT_WIDTH, D_MODEL),
            _layer_spec(layer, D_MODEL, D_MODEL),
        ],
        out_specs=row(D_MODEL),
        out_shape=jax.ShapeDtypeStruct((b, l, D_MODEL), F32),
        compiler_params=_cparams(("arbitrary", "arbitrary")),
        name="merge",
    )(x, mod, of, ob, za, yb, ga, gb, gnw, wpa, wpb, wout)


def _pick_tile(l, pref):
    t = min(pref, l)
    while l % t:
        t //= 2
    return t


def kernel(x, c, ctx, c_ctx, norm_w, w_mod, b_mod, w_in, conv_w, a_log, dt_bias, gdn_norm_w,
           q_norm_w, k_norm_w, sink, w_proj_a, w_proj_b, w_out):
    b, l, _ = x.shape
    lc = ctx.shape[1]
    assert l % ATT_BLOCK == 0 and lc % ATT_BLOCK == 0 and x.shape[2] == D_MODEL

    rows = ((b + 1 + 7) // 8) * 8
    c_all = jnp.concatenate([c, c_ctx[None, :], jnp.zeros((rows - b - 1, D_MODEL), F32)], axis=0)
    mod = _mod_call(c_all, w_mod, b_mod).reshape(DEPTH * 3 * rows, 1, D_MODEL)

    cos_l, sin_l = _rope_tables(l)
    cos_c = jnp.ones((lc, LANES), F32)
    sin_c = jnp.zeros((lc, LANES), F32)
    grp = (np.arange(LANES) // ROPE_F) % ATT_KV_HEADS
    gmat = jnp.asarray((grp[:, None] == grp[None, :]).astype(np.float32), dtype=BF16)
    s0 = jnp.zeros((b, 2, GDN_HEADS, LANES, LANES), F32)
    tm_l = _pick_tile(l, 512)
    tm_c = _pick_tile(lc, 256)
    tc_l = _pick_tile(l, 256)
    tc_c = _pick_tile(lc, 256)
    tch_l = _pick_tile(l, 512)

    w_packed_all = _pack_w_in(w_in)
    ng = 2 * GDN_HEADS
    gpad = jnp.zeros((DEPTH, ng), F32)
    gvec_all = jnp.concatenate([
        jnp.concatenate([gpad, jnp.exp(a_log.astype(F32)).reshape(DEPTH, ng),
                         jnp.zeros((DEPTH, LANES - 2 * ng), F32)], axis=1)[:, None, :],
        jnp.concatenate([gpad, dt_bias.astype(F32).reshape(DEPTH, ng),
                         jnp.zeros((DEPTH, LANES - 2 * ng), F32)], axis=1)[:, None, :],
        jnp.zeros((DEPTH, 6, LANES), F32)], axis=1)
    qn_all = (jax.vmap(_norm_lanes)(q_norm_w) * (ATT_HD ** -0.5 * LOG2E))[:, None, :]
    kn_all = jax.vmap(_norm_lanes)(k_norm_w)[:, None, :]
    cw_all = jnp.pad(conv_w, ((0, 0), (0, 8 - SHORT_CONV), (0, 0)))
    normw_all = norm_w[:, None, :]
    gnw_all = gdn_norm_w[:, None, :]
    wpa_all = w_proj_a.astype(BF16)
    wpb_all = jnp.swapaxes(w_proj_b.reshape(DEPTH, ATT_KV_HEADS, ATT_GROUP, ATT_HD, D_MODEL), 1, 2).reshape(
        DEPTH, ATT_WIDTH, D_MODEL).astype(BF16)
    wout_all = w_out.astype(BF16)
    proj_params = (normw_all, w_packed_all, gvec_all, qn_all, kn_all, gmat)
    merge_params = (gnw_all, wpa_all, wpb_all, wout_all)

    for i in range(DEPTH):
        update_ctx = i < DEPTH - 1

        outs_c = _proj_call(ctx, mod, b, i, *proj_params, cos_c, sin_c, with_rest=update_ctx, tm=tm_c)
        qkv_c, gates_c, kv_c = outs_c[:3]
        prep_c = _prep_call(qkv_c, cw_all, i, gates_c, tc=tc_c)
        of_c, ob_c, s_ctx = _chain_call(*prep_c, s0, tc=tc_c)

        qkv_l, gates_l, kv_l, za_l, q_l, zb_l, ga_l, gb_l = _proj_call(
            x, mod, None, i, *proj_params, cos_l, sin_l, with_rest=True, tm=tm_l)
        prep_l = _prep_call(qkv_l, cw_all, i, gates_l, tc=tc_l)
        of_l, ob_l, _ = _chain_call(*prep_l, s_ctx, tc=tch_l)
        yb_l = _attn_call(sink, i, q_l, kv_l, kv_c, zb_l, local=True)
        x_new = _merge_call(x, mod, None, i, of_l, ob_l, za_l, yb_l, ga_l, gb_l, *merge_params, tm=tm_l)

        if update_ctx:
            za_c, q_c, zb_c, ga_c, gb_c = outs_c[3:]
            yb_c = _attn_call(sink, i, q_c, kv_c, kv_c, zb_c, local=False)
            ctx = _merge_call(ctx, mod, b, i, of_c, ob_c, za_c, yb_c, ga_c, gb_c, *merge_params, tm=tm_c)
        x = x_new
    return x
```

```python
import functools
import math

import numpy as np
import jax
import jax.numpy as jnp
from jax import lax
from jax.experimental import pallas as pl
from jax.experimental.pallas import tpu as pltpu

F32 = jnp.float32
BF16 = jnp.bfloat16

D_MODEL = 1024
DEPTH = 2
GRID_W = 64
EPS = 1e-6
NEG_INF = -1e30

GDN_HEADS = 4
GDN_DK = 128
GDN_DV = 128
GDN_QKV = GDN_HEADS * (2 * GDN_DK + GDN_DV)
GDN_WIDTH = GDN_HEADS * GDN_DV
SHORT_CONV = 5
CHUNK = 64

ATT_HEADS = 8
ATT_KV_HEADS = 2
ATT_GROUP = ATT_HEADS // ATT_KV_HEADS
ATT_HD = 64
ATT_WIDTH = ATT_HEADS * ATT_HD
ATT_KV_WIDTH = ATT_KV_HEADS * ATT_HD
ATT_BLOCK = 128
ROPE_BASE = 10000.0
AXIS_DIM = ATT_HD // 2
ROPE_F = AXIS_DIM // 2
LOG2E = math.log2(math.e)

N_STATE = GDN_QKV + 4 * GDN_HEADS + 2 * ATT_KV_WIDTH

LANES = 128
P_QKV = 0
P_GATE = GDN_QKV
P_K = P_GATE + LANES
P_V = P_K + ATT_KV_WIDTH
P_STATE_END = P_V + ATT_KV_WIDTH
P_ZA = P_STATE_END
P_Q = P_ZA + GDN_WIDTH
P_ZB = P_Q + ATT_WIDTH
P_GA = P_ZB + ATT_WIDTH
P_GB = P_GA + D_MODEL
P_END = P_GB + D_MODEL

VMEM_LIMIT = 56 * 1024 * 1024


def _cparams(sem):
    return pltpu.CompilerParams(dimension_semantics=sem, vmem_limit_bytes=VMEM_LIMIT)


def _k_cols(w):
    lead = w.shape[:-1]
    w = w.reshape(lead + (ATT_KV_HEADS, 4, ROPE_F))
    return jnp.swapaxes(w, -3, -2).reshape(lead + (ATT_KV_WIDTH,))


def _q_cols(w):
    lead = w.shape[:-1]
    w = w.reshape(lead + (ATT_KV_HEADS, ATT_GROUP, 4, ROPE_F))
    return jnp.moveaxis(w, -4, -2).reshape(lead + (ATT_WIDTH,))


def _o_cols(w):
    lead = w.shape[:-1]
    w = w.reshape(lead + (ATT_KV_HEADS, ATT_GROUP, ATT_HD))
    return jnp.swapaxes(w, -3, -2).reshape(lead + (ATT_WIDTH,))


def _norm_lanes(w):
    w = w.reshape(4, 1, ROPE_F)
    return jnp.broadcast_to(w, (4, ATT_KV_HEADS, ROPE_F)).reshape(LANES)


def _pack_w_in(w_in):
    w_in = w_in.astype(BF16)
    o = 0
    qkv = w_in[..., o:o + GDN_QKV]; o += GDN_QKV
    gates = w_in[..., o:o + 4 * GDN_HEADS]; o += 4 * GDN_HEADS
    kb = w_in[..., o:o + ATT_KV_WIDTH]; o += ATT_KV_WIDTH
    vb = w_in[..., o:o + ATT_KV_WIDTH]; o += ATT_KV_WIDTH
    za = w_in[..., o:o + GDN_WIDTH]; o += GDN_WIDTH
    qb = w_in[..., o:o + ATT_WIDTH]; o += ATT_WIDTH
    zb = w_in[..., o:o + ATT_WIDTH]; o += ATT_WIDTH
    ga = w_in[..., o:o + D_MODEL]; o += D_MODEL
    gb = w_in[..., o:o + D_MODEL]
    gates = jnp.concatenate([gates, jnp.zeros(gates.shape[:-1] + (LANES - 4 * GDN_HEADS,), BF16)], axis=-1)
    return jnp.concatenate(
        [qkv, gates, _k_cols(kb), vb, za, _q_cols(qb), _o_cols(zb), ga, gb], axis=-1)


def _rope_tables(seq_len):
    t = jnp.arange(seq_len, dtype=F32)
    pos_row = jnp.floor(t / GRID_W)
    pos_col = t - pos_row * GRID_W
    inv = ROPE_BASE ** (-jnp.arange(0, AXIS_DIM, 2, dtype=F32) / AXIS_DIM)
    ang_r = pos_row[:, None] * inv[None, :]
    ang_c = pos_col[:, None] * inv[None, :]

    def lay(a):
        return jnp.concatenate([a, a], axis=1)

    cos = jnp.concatenate([lay(jnp.cos(ang_r)), lay(jnp.cos(ang_r)), lay(jnp.cos(ang_c)), lay(jnp.cos(ang_c))], axis=1)
    sin = jnp.concatenate([-lay(jnp.sin(ang_r)), lay(jnp.sin(ang_r)), -lay(jnp.sin(ang_c)), lay(jnp.sin(ang_c))], axis=1)
    return cos, sin


def _mod_kernel(c_ref, w_ref, b_ref, o_ref):
    c = c_ref[...]
    s = c * jax.nn.sigmoid(c)
    o_ref[0, 0] = jnp.dot(s.astype(BF16), w_ref[0].astype(BF16), preferred_element_type=F32) + b_ref[0]


def _mod_call(c_all, w_mod, b_mod):
    rows = c_all.shape[0]
    nblk = 3
    return pl.pallas_call(
        _mod_kernel,
        grid=(DEPTH, nblk),
        in_specs=[
            pl.BlockSpec((rows, D_MODEL), lambda l, j: (0, 0)),
            pl.BlockSpec((1, D_MODEL, D_MODEL), lambda l, j: (l, 0, j)),
            pl.BlockSpec((1, 1, D_MODEL), lambda l, j: (l, 0, j)),
        ],
        out_specs=pl.BlockSpec((1, 1, rows, D_MODEL), lambda l, j: (l, j, 0, 0)),
        out_shape=jax.ShapeDtypeStruct((DEPTH, nblk, rows, D_MODEL), F32),
        compiler_params=_cparams(("arbitrary", "arbitrary")),
        name="mod",
    )(c_all, w_mod, b_mod.reshape(DEPTH, 1, 3 * D_MODEL))


def _mod_spec(layer, kind, rows, row):
    base = (layer * 3 + kind) * rows
    if row is None:
        return pl.BlockSpec((1, 1, D_MODEL), lambda i, j: (base + i, 0, 0))
    return pl.BlockSpec((1, 1, D_MODEL), lambda i, j: (base + row, 0, 0))


def _layer_spec(layer, *block):
    zeros = (0,) * len(block)
    return pl.BlockSpec((None,) + tuple(block), lambda i, j: (layer,) + zeros)


def _swap32(t):
    lane = lax.broadcasted_iota(jnp.int32, t.shape, 1)
    even = (lane // 32) % 2 == 0
    return jnp.where(even, pltpu.roll(t, 96, 1), pltpu.roll(t, 32, 1))


def _headnorm_rope(parts, nws, gmat, cos, sin):
    t = jnp.concatenate(parts, axis=0) if len(parts) > 1 else parts[0]
    ss = jnp.dot((t * t).astype(BF16), gmat, preferred_element_type=F32)
    inv = lax.rsqrt(ss * (1.0 / ATT_HD) + EPS)
    tm = parts[0].shape[0]
    outs = []
    for i, nw in enumerate(nws):
        tn = parts[i] * inv[i * tm:(i + 1) * tm] * nw
        outs.append(tn * cos + _swap32(tn) * sin)
    return outs


def _softplus(x):
    return jnp.maximum(x, 0.0) + jnp.log1p(jnp.exp(-jnp.abs(x)))


def _proj_kernel(x_ref, scale_ref, shift_ref, normw_ref, w_ref, gvec_ref, qn_ref, kn_ref, gmat_ref,
                 cos_ref, sin_ref, *out_refs, with_rest):
    if with_rest:
        qkv_ref, gates_ref, kv_ref, za_ref, q_ref, zb_ref, ga_ref, gb_ref = out_refs
    else:
        qkv_ref, gates_ref, kv_ref = out_refs
    x = x_ref[0]
    ms = jnp.mean(x * x, axis=-1, keepdims=True)
    a = normw_ref[...] * (1.0 + scale_ref[0])
    h = (x * lax.rsqrt(ms + EPS) * a + shift_ref[0]).astype(BF16)

    def proj(lo, width):
        return jnp.dot(h, w_ref[:, lo:lo + width], preferred_element_type=F32)

    for c in range(GDN_QKV // 512):
        qkv_ref[0, :, c * 512:(c + 1) * 512] = proj(P_QKV + c * 512, 512).astype(BF16)

    g = proj(P_GATE, LANES)
    lane = lax.broadcasted_iota(jnp.int32, g.shape, 1)
    beta = jax.nn.sigmoid(g)
    dec = -gvec_ref[0:1, :] * _softplus(g + gvec_ref[1:2, :])
    gates_ref[0] = jnp.where(lane < 2 * GDN_HEADS, beta, dec)

    gmat = gmat_ref[...]
    cos = cos_ref[...]
    sin = sin_ref[...]
    kv_ref[0, :, LANES:2 * LANES] = proj(P_V, LANES).astype(BF16)
    parts = [proj(P_K, LANES)]
    nws = [kn_ref[...]]
    if with_rest:
        qraw = proj(P_Q, ATT_WIDTH)
        parts += [qraw[:, gi * LANES:(gi + 1) * LANES] for gi in range(ATT_GROUP)]
        nws += [qn_ref[...]] * ATT_GROUP
    normed = _headnorm_rope(parts, nws, gmat, cos, sin)
    kv_ref[0, :, 0:LANES] = normed[0].astype(BF16)

    if with_rest:
        for gi in range(ATT_GROUP):
            q_ref[0, :, gi * LANES:(gi + 1) * LANES] = normed[1 + gi].astype(BF16)
        z = proj(P_ZA, GDN_WIDTH)
        za_ref[0] = (z * jax.nn.sigmoid(z)).astype(BF16)
        z = proj(P_ZB, ATT_WIDTH)
        zb_ref[0] = (z * jax.nn.sigmoid(z)).astype(BF16)
        for c in range(D_MODEL // 512):
            ga_ref[0, :, c * 512:(c + 1) * 512] = jax.nn.sigmoid(proj(P_GA + c * 512, 512)).astype(BF16)
            gb_ref[0, :, c * 512:(c + 1) * 512] = jax.nn.sigmoid(proj(P_GB + c * 512, 512)).astype(BF16)


def _proj_call(x, mod, mod_row, layer, normw, w_packed, gvec, qn, kn, gmat, cos, sin, *, with_rest, tm):
    b, l, _ = x.shape
    nw = P_END if with_rest else P_STATE_END
    rows = mod.shape[0] // (3 * DEPTH)
    const2 = lambda i, j: (0, 0)
    row_map = lambda i, j: (i, j, 0)
    in_specs = [
        pl.BlockSpec((1, tm, D_MODEL), row_map),
        _mod_spec(layer, 1, rows, mod_row),
        _mod_spec(layer, 0, rows, mod_row),
        _layer_spec(layer, 1, D_MODEL),
        pl.BlockSpec((None, D_MODEL, nw), lambda i, j: (layer, 0, 0), pipeline_mode=pl.Buffered(1)),
        _layer_spec(layer, 8, LANES),
        _layer_spec(layer, 1, LANES),
        _layer_spec(layer, 1, LANES),
        pl.BlockSpec((LANES, LANES), const2),
        pl.BlockSpec((tm, LANES), lambda i, j: (j, 0)),
        pl.BlockSpec((tm, LANES), lambda i, j: (j, 0)),
    ]
    widths = [(GDN_QKV, BF16), (LANES, F32), (2 * LANES, BF16)]
    if with_rest:
        widths += [(GDN_WIDTH, BF16), (ATT_WIDTH, BF16), (ATT_WIDTH, BF16), (D_MODEL, BF16), (D_MODEL, BF16)]
    out_specs = [pl.BlockSpec((1, tm, w), row_map) for w, _ in widths]
    out_shape = [jax.ShapeDtypeStruct((b, l, w), dt) for w, dt in widths]
    return pl.pallas_call(
        functools.partial(_proj_kernel, with_rest=with_rest),
        grid=(b, l // tm),
        in_specs=in_specs,
        out_specs=out_specs,
        out_shape=out_shape,
        compiler_params=_cparams(("arbitrary", "arbitrary")),
        name="proj_full" if with_rest else "proj_state",
    )(x, mod, mod, normw, w_packed, gvec, qn, kn, gmat, cos, sin)


HALO = 16


def _blockdiag(y, isf):
    zero = jnp.zeros_like(y)
    return jnp.concatenate([jnp.where(isf, y, zero), jnp.where(isf, zero, y)], axis=0)


def _short_conv_stage(raw_ref, prev_ref, next_ref, cw_ref, raw_scr, qkv_scr, tc):
    j = pl.program_id(1)
    n = pl.num_programs(1)
    half = SHORT_CONV // 2
    for cb in range(GDN_QKV // LANES):
        sl = slice(cb * LANES, (cb + 1) * LANES)
        raw_scr[cb, 0:HALO, :] = jnp.where(j > 0, prev_ref[0, :, sl].astype(F32), 0.0)
        raw_scr[cb, HALO:HALO + tc, :] = raw_ref[0, :, sl].astype(F32)
        raw_scr[cb, HALO + tc:2 * HALO + tc, :] = jnp.where(j < n - 1, next_ref[0, :, sl].astype(F32), 0.0)
        y = raw_scr[cb, HALO - half:HALO - half + tc, :] * cw_ref[0:1, sl]
        for tap in range(1, SHORT_CONV):
            y = y + raw_scr[cb, HALO - half + tap:HALO - half + tap + tc, :] * cw_ref[tap:tap + 1, sl]
        y = y * jax.nn.sigmoid(y)
        if cb < 2 * GDN_HEADS:
            scale = GDN_DK ** -0.5 if cb < GDN_HEADS else 1.0
            y = y * (lax.rsqrt(jnp.sum(y * y, axis=-1, keepdims=True) + EPS) * scale)
        qkv_scr[:, sl] = y.astype(BF16)


def _prep_kernel(raw_ref, prev_ref, next_ref, cw_ref, gates_ref, u_ref, w_ref, kdt_ref, qd_ref, qk_ref, eg_ref,
                 raw_scr, qkv_ref, a_scr, t_scr, rhs_scr, *, tc):
    H = GDN_HEADS
    C = CHUNK

    @pl.when(jnp.logical_and(pl.program_id(0) == 0, pl.program_id(1) == 0))
    def _():
        rhs_scr[...] = jnp.zeros(rhs_scr.shape, BF16)

    _short_conv_stage(raw_ref, prev_ref, next_ref, cw_ref, raw_scr, qkv_ref, tc)
    G = gates_ref[0]
    gT = G.T[2 * H:4 * H, :]
    lane_t = lax.broadcasted_iota(jnp.int32, gT.shape, 1)
    pos = lane_t % C
    pre = gT
    suf = gT
    for s in (1, 2, 4, 8, 16, 32):
        pre = pre + jnp.where(pos >= s, pltpu.roll(pre, s, 1), 0.0)
        suf = suf + jnp.where(pos < C - s, pltpu.roll(suf, tc - s, 1), 0.0)
    row8 = lax.broadcasted_iota(jnp.int32, gT.shape, 0)
    gcT = jnp.where(row8 < H, pre, suf)
    gc = jnp.concatenate([gcT, jnp.zeros((LANES - 2 * H, tc), F32)], axis=0).T

    lane = lax.broadcasted_iota(jnp.int32, (C, LANES), 1)
    ii = lax.broadcasted_iota(jnp.int32, (C, LANES), 0)
    jj = lane % C
    isf = lane < C
    isb = jnp.logical_not(isf)
    incl = jnp.logical_or(jnp.logical_and(isf, ii >= jj), jnp.logical_and(isb, ii <= jj))
    strict = jnp.logical_or(jnp.logical_and(isf, ii > jj), jnp.logical_and(isb, ii < jj))
    eye2 = (ii == jj).astype(F32)
    same = {kk: (ii // kk) == (jj // kk) for kk in (2, 4, 8, 16, 32, 64)}
    lane8 = lax.broadcasted_iota(jnp.int32, (1, LANES), 1)

    def colb(arr, r0, c):
        return jnp.broadcast_to(arr[r0:r0 + C, c:c + 1], (C, LANES))

    for s in range(tc // C):
        r0 = s * C
        glrow = jnp.where(lane8 < H, gc[r0 + C - 1:r0 + C, :], gc[r0:r0 + 1, :])
        eg_ref[0, s] = jnp.broadcast_to(jnp.exp(glrow), (8, LANES))

    items = [(lt, 2 * p + hp, halfsel) for lt in range(tc // LANES) for p in range(H // 2)
             for halfsel in range(2) for hp in range(2)]
    gts = {}
    for lt in range(tc // LANES):
        gt = gcT[:, lt * LANES:(lt + 1) * LANES]
        gts[lt] = (gt, pltpu.roll(gt, C, 1))

    kd_parts = {}
    qk_hold = {}

    def stage1(idx):
        lt, h, halfsel = items[idx]
        gt, gt_r = gts[lt]
        r0 = (2 * lt + halfsel) * C
        if halfsel == 0:
            row_f, row_b = gt[h:h + 1, :], gt_r[H + h:H + h + 1, :]
        else:
            row_f, row_b = gt_r[h:h + 1, :], gt[H + h:H + h + 1, :]
        gc_row2 = jnp.broadcast_to(jnp.where(lane8 < C, row_f, row_b), (C, LANES))
        gcf = colb(gc, r0, h)
        gcb = colb(gc, r0, H + h)
        dec = jnp.where(incl, jnp.exp(jnp.where(incl, jnp.where(isf, gcf, gcb) - gc_row2, 0.0)), 0.0)
        bf = colb(G, r0, h)
        bb = colb(G, r0, H + h)
        qt = qkv_ref[r0:r0 + C, h * LANES:(h + 1) * LANES]
        kt = qkv_ref[r0:r0 + C, (H + h) * LANES:(H + h + 1) * LANES]
        vt = qkv_ref[r0:r0 + C, (2 * H + h) * LANES:(2 * H + h + 1) * LANES]
        kq = lax.dot_general(jnp.concatenate([kt, qt], axis=0), jnp.concatenate([kt, kt], axis=0),
                             (((1,), (1,)), ((), ())), preferred_element_type=F32)
        a2 = jnp.where(strict, jnp.where(isf, bf, bb) * kq[0:C] * dec, 0.0)
        a_scr[idx] = a2
        t_scr[idx] = eye2 - jnp.where(same[2], a2, 0.0)
        qk2 = kq[C:2 * C] * dec
        if h % 2 == 0:
            qk_hold[(lt, h, halfsel)] = qk2
        else:
            qk_even = qk_hold.pop((lt, h - 1, halfsel))
            pc = (h // 2) * LANES
            qk_ref[0, 0, r0:r0 + C, pc:pc + LANES] = jnp.where(isf, qk_even, pltpu.roll(qk2, C, 1)).astype(BF16)
            qk_ref[0, 1, r0:r0 + C, pc:pc + LANES] = jnp.where(isf, pltpu.roll(qk_even, C, 1), qk2).astype(BF16)

        kf = kt.astype(F32)
        vf = vt.astype(F32)
        qf = qt.astype(F32)
        egf = jnp.exp(gcf)
        egb = jnp.exp(gcb)
        rhs_scr[idx, 0:C, 0:2 * LANES] = jnp.concatenate([vf * bf, kf * (bf * egf)], axis=1).astype(BF16)
        rhs_scr[idx, C:2 * C, 2 * LANES:4 * LANES] = jnp.concatenate([vf * bb, kf * (bb * egb)], axis=1).astype(BF16)
        cf = h * LANES
        qd_ref[0, 0, r0:r0 + C, cf:cf + LANES] = (qf * egf).astype(BF16)
        qd_ref[0, 1, r0:r0 + C, cf:cf + LANES] = (qf * egb).astype(BF16)
        glf = jnp.broadcast_to(gc[r0 + C - 1:r0 + C, h:h + 1], (C, LANES))
        glb = jnp.broadcast_to(gc[r0:r0 + 1, H + h:H + h + 1], (C, LANES))
        kd_parts[(lt, h, halfsel)] = (kf * jnp.exp(glf - gcf), kf * jnp.exp(glb - gcb))
        if halfsel == 1:
            for d in range(2):
                kdt = jnp.concatenate([kd_parts[(lt, h, 0)][d], kd_parts[(lt, h, 1)][d]], axis=0).T
                kdt_ref[0, 0, (d * H + h) * LANES:(d * H + h + 1) * LANES, lt * LANES:(lt + 1) * LANES] = kdt.astype(BF16)

    levels = (2, 4, 8, 16, 32)

    def round_a(kk, idxs):
        emask = jnp.logical_and(same[2 * kk], jnp.logical_not(same[kk]))
        ps = []
        for idx in idxs:
            e = jnp.where(emask, a_scr[idx], 0.0).astype(BF16)
            ps.append(jnp.dot(e, _blockdiag(t_scr[idx].astype(BF16), isf), preferred_element_type=F32))
        return ps

    def round_b(idxs, ps):
        for idx, p in zip(idxs, ps):
            t2 = t_scr[idx]
            t_scr[idx] = t2 - jnp.dot(t2.astype(BF16), _blockdiag(p.astype(BF16), isf),
                                      preferred_element_type=F32)

    def stage3(idx):
        lt, h, halfsel = items[idx]
        r0 = (2 * lt + halfsel) * C
        uw = jnp.dot(t_scr[idx].astype(BF16), rhs_scr[idx], preferred_element_type=F32)
        cf = h * LANES
        u_ref[0, 0, r0:r0 + C, cf:cf + LANES] = uw[:, 0:LANES].astype(BF16)
        w_ref[0, 0, r0:r0 + C, cf:cf + LANES] = uw[:, LANES:2 * LANES].astype(BF16)
        u_ref[0, 1, r0:r0 + C, cf:cf + LANES] = uw[:, 2 * LANES:3 * LANES].astype(BF16)
        w_ref[0, 1, r0:r0 + C, cf:cf + LANES] = uw[:, 3 * LANES:4 * LANES].astype(BF16)

    everything = list(range(len(items)))
    for idx in everything:
        stage1(idx)
    for kk in levels:
        round_b(everything, round_a(kk, everything))
    for idx in everything:
        stage3(idx)


def _prep_call(qkv_raw, conv_w, layer, gates, *, tc):
    b, l, _ = qkv_raw.shape
    row_map = lambda i, j: (i, j, 0)
    wide = 2 * GDN_HEADS * LANES
    hw = GDN_HEADS * LANES
    dir_map = lambda i, j: (i, 0, j, 0)
    n_items = (tc // CHUNK) * GDN_HEADS
    hpt = tc // HALO
    nhalo = l // HALO
    n = l // tc
    return pl.pallas_call(
        functools.partial(_prep_kernel, tc=tc),
        grid=(b, l // tc),
        in_specs=[
            pl.BlockSpec((1, tc, GDN_QKV), row_map),
            pl.BlockSpec((1, HALO, GDN_QKV), lambda i, j: (i, jnp.maximum(j * hpt - 1, 0), 0)),
            pl.BlockSpec((1, HALO, GDN_QKV), lambda i, j: (i, jnp.minimum((j + 1) * hpt, nhalo - 1), 0)),
            _layer_spec(layer, 8, GDN_QKV),
            pl.BlockSpec((1, tc, LANES), row_map),
        ],
        out_specs=[
            pl.BlockSpec((1, 2, tc, hw), dir_map),
            pl.BlockSpec((1, 2, tc, hw), dir_map),
            pl.BlockSpec((1, 1, wide, tc), lambda i, j: (i, j, 0, 0)),
            pl.BlockSpec((1, 2, tc, hw), dir_map),
            pl.BlockSpec((1, 2, tc, GDN_HEADS * CHUNK), dir_map),
            pl.BlockSpec((1, tc // CHUNK, 8, LANES), lambda i, j: (i, j, 0, 0)),
        ],
        out_shape=[
            jax.ShapeDtypeStruct((b, 2, l, hw), BF16),
            jax.ShapeDtypeStruct((b, 2, l, hw), BF16),
            jax.ShapeDtypeStruct((b, l // tc, wide, tc), BF16),
            jax.ShapeDtypeStruct((b, 2, l, hw), BF16),
            jax.ShapeDtypeStruct((b, 2, l, GDN_HEADS * CHUNK), BF16),
            jax.ShapeDtypeStruct((b, l // CHUNK, 8, LANES), F32),
        ],
        scratch_shapes=[
            pltpu.VMEM((GDN_QKV // LANES, tc + 2 * HALO, LANES), F32),
            pltpu.VMEM((tc, GDN_QKV), BF16),
            pltpu.VMEM((n_items, CHUNK, LANES), F32),
            pltpu.VMEM((n_items, CHUNK, LANES), F32),
            pltpu.VMEM((n_items, 2 * CHUNK, 4 * LANES), BF16),
        ],
        compiler_params=_cparams(("arbitrary", "arbitrary")),
        name="gdn_prep",
    )(qkv_raw, qkv_raw, qkv_raw, conv_w, gates)


def _chain_kernel(uf_ref, wf_ref, kdtf_ref, qdf_ref, qkf_ref, egf_ref,
                  ub_ref, wb_ref, kdtb_ref, qdb_ref, qkb_ref, egb_ref, s0_ref,
                  of_ref, ob_ref, sfin_ref, s_scr, *, tc, tcp):
    H = GDN_HEADS
    C = CHUNK
    i = pl.program_id(1)
    n = pl.num_programs(1)

    @pl.when(i == 0)
    def _():
        s_scr[...] = s0_ref[0]

    nch = tc // C
    zc = jnp.zeros((C, LANES), BF16)
    refs = ((uf_ref, wf_ref, kdtf_ref, qdf_ref, qkf_ref, egf_ref, of_ref),
            (ub_ref, wb_ref, kdtb_ref, qdb_ref, qkb_ref, egb_ref, ob_ref))
    chains = [(d, h) for d in range(2) for h in range(H)]
    for c in range(nch):
        rs = []
        for d, h in chains:
            u_ref, w_ref, kdt_ref, qd_ref, qk_ref, eg_ref, o_ref = refs[d]
            r0 = (c if d == 0 else nch - 1 - c) * C
            cs = h * LANES
            lhs = jnp.concatenate([w_ref[0, 0, r0:r0 + C, cs:cs + LANES], qd_ref[0, 0, r0:r0 + C, cs:cs + LANES]],
                                  axis=0)
            rs.append(jnp.dot(lhs, s_scr[d, h].astype(BF16), preferred_element_type=F32))
        vbs = []
        for (d, h), r in zip(chains, rs):
            u_ref, w_ref, kdt_ref, qd_ref, qk_ref, eg_ref, o_ref = refs[d]
            r0 = (c if d == 0 else nch - 1 - c) * C
            cs = h * LANES
            vbs.append((u_ref[0, 0, r0:r0 + C, cs:cs + LANES].astype(F32) - r[0:C]).astype(BF16))
        for (d, h), vb in zip(chains, vbs):
            u_ref, w_ref, kdt_ref, qd_ref, qk_ref, eg_ref, o_ref = refs[d]
            cc = c if d == 0 else nch - 1 - c
            pt, within = divmod(cc * C, tcp)
            lt, par = divmod(within // C, 2)
            cs = h * LANES
            v_par = jnp.concatenate([vb, zc], axis=0) if par == 0 else jnp.concatenate([zc, vb], axis=0)
            kdt = kdt_ref[0, pt, cs:cs + LANES, lt * LANES:(lt + 1) * LANES]
            eg = jnp.broadcast_to(eg_ref[0, cc, 0:1, d * H + h:d * H + h + 1], (LANES, LANES))
            s_scr[d, h] = s_scr[d, h] * eg + jnp.dot(kdt, v_par, preferred_element_type=F32)
        for d in range(2):
            u_ref, w_ref, kdt_ref, qd_ref, qk_ref, eg_ref, o_ref = refs[d]
            r0 = (c if d == 0 else nch - 1 - c) * C
            for p in range(H // 2):
                v0, v1 = vbs[d * H + 2 * p], vbs[d * H + 2 * p + 1]
                v_pair = jnp.concatenate([jnp.concatenate([v0, zc], axis=1), jnp.concatenate([zc, v1], axis=1)], axis=0)
                intra = jnp.dot(qk_ref[0, 0, r0:r0 + C, p * LANES:(p + 1) * LANES], v_pair,
                                preferred_element_type=F32)
                for hp in range(2):
                    h = 2 * p + hp
                    o = rs[d * H + h][C:2 * C] + intra[:, hp * LANES:(hp + 1) * LANES]
                    o_ref[0, r0:r0 + C, h * LANES:(h + 1) * LANES] = o.astype(BF16)

    @pl.when(i == n - 1)
    def _():
        sfin_ref[0] = s_scr[...]


def _chain_call(u, w, kdt, qd, qk, eg, s0, *, tc):
    b, _, l, hw = u.shape
    tcp = kdt.shape[3]
    assert tc % tcp == 0
    tpr = tc // tcp
    n = l // tc
    fwd = lambda i, j: (i, 0, j, 0)
    bwd = lambda i, j: (i, 1, n - 1 - j, 0)
    fwd3 = lambda i, j: (i, j, 0)
    bwd3 = lambda i, j: (i, n - 1 - j, 0)
    qkw = GDN_HEADS * CHUNK
    nchunk = tc // CHUNK
    in_specs = [
        pl.BlockSpec((1, 1, tc, hw), fwd), pl.BlockSpec((1, 1, tc, hw), fwd),
        pl.BlockSpec((1, tpr, hw, tcp), lambda i, j: (i, j, 0, 0)),
        pl.BlockSpec((1, 1, tc, hw), fwd), pl.BlockSpec((1, 1, tc, qkw), fwd),
        pl.BlockSpec((1, nchunk, 8, LANES), lambda i, j: (i, j, 0, 0)),
        pl.BlockSpec((1, 1, tc, hw), bwd), pl.BlockSpec((1, 1, tc, hw), bwd),
        pl.BlockSpec((1, tpr, hw, tcp), lambda i, j: (i, n - 1 - j, 1, 0)),
        pl.BlockSpec((1, 1, tc, hw), bwd), pl.BlockSpec((1, 1, tc, qkw), bwd),
        pl.BlockSpec((1, nchunk, 8, LANES), lambda i, j: (i, n - 1 - j, 0, 0)),
        pl.BlockSpec((1, 2, GDN_HEADS, LANES, LANES), lambda i, j: (i, 0, 0, 0, 0)),
    ]
    out_specs = [
        pl.BlockSpec((1, tc, hw), fwd3),
        pl.BlockSpec((1, tc, hw), bwd3),
        pl.BlockSpec((1, 2, GDN_HEADS, LANES, LANES), lambda i, j: (i, 0, 0, 0, 0)),
    ]
    out_shape = [
        jax.ShapeDtypeStruct((b, l, hw), BF16),
        jax.ShapeDtypeStruct((b, l, hw), BF16),
        jax.ShapeDtypeStruct((b, 2, GDN_HEADS, LANES, LANES), F32),
    ]
    return pl.pallas_call(
        functools.partial(_chain_kernel, tc=tc, tcp=tcp),
        grid=(b, n),
        in_specs=in_specs,
        out_specs=out_specs,
        out_shape=out_shape,
        scratch_shapes=[pltpu.VMEM((2, GDN_HEADS, LANES, LANES), F32)],
        compiler_params=_cparams(("arbitrary", "arbitrary")),
        name="gdn_chain",
    )(u, w, kdt, qd, qk, eg, u, w, kdt, qd, qk, eg, s0)


def _attn_kernel(sink_ref, q_ref, kvp_ref, kvo_ref, kvn_ref, kvc_ref, zb_ref, o_ref, *, local, layer):
    nb = pl.num_programs(1)
    n = pl.program_id(1)
    T = ATT_BLOCK
    q = q_ref[0]
    qs = jnp.concatenate([q[:, g * LANES:(g + 1) * LANES] for g in range(ATT_GROUP)], axis=0)
    lane = lax.broadcasted_iota(jnp.int32, (1, LANES), 1)
    rowi = lax.broadcasted_iota(jnp.int32, (ATT_GROUP * T, T), 0) % T
    colj = lax.broadcasted_iota(jnp.int32, (ATT_GROUP * T, T), 1)
    grp = lax.broadcasted_iota(jnp.int32, (ATT_GROUP * T, 1), 0) // T

    pieces = [(kvc_ref, None)]
    if local:
        pieces += [(kvp_ref, jnp.logical_and(colj >= rowi, n > 0)),
                   (kvo_ref, None),
                   (kvn_ref, jnp.logical_and(colj <= rowi, n < nb - 1))]

    heads = range(ATT_KV_HEADS)
    vcats, tiles = [], []
    for kvh in heads:
        kmask = ((lane // ROPE_F) % ATT_KV_HEADS == kvh)
        vmask = (lane // ATT_HD == kvh)
        k_rows, v_rows = [], []
        for ref, _ in pieces:
            kv = ref[0]
            zero = jnp.zeros_like(kv[:, 0:LANES])
            k_rows.append(jnp.where(kmask, kv[:, 0:LANES], zero))
            v_rows.append(jnp.where(vmask, kv[:, LANES:2 * LANES], zero))
        vcats.append(jnp.concatenate(v_rows, axis=0))
        s_h = lax.dot_general(qs, jnp.concatenate(k_rows, axis=0), (((1,), (1,)), ((), ())),
                              preferred_element_type=F32)
        th, col = [], 0
        for ref, valid in pieces:
            for c in range(ref.shape[1] // LANES):
                t = s_h[:, col:col + LANES]
                th.append(t if valid is None else jnp.where(valid, t, NEG_INF))
                col += LANES
        tiles.append(th)
    sinks, ms = [], []
    for kvh in heads:
        sink = jnp.zeros((ATT_GROUP * T, 1), F32)
        for g in range(ATT_GROUP):
            sink = jnp.where(grp == g, sink_ref[layer, kvh * ATT_GROUP + g] * LOG2E, sink)
        sinks.append(sink)
        mt = tiles[kvh][0]
        for t in tiles[kvh][1:]:
            mt = jnp.maximum(mt, t)
        ms.append(jnp.maximum(sink, jnp.max(mt, axis=-1, keepdims=True)))
    pcats, inv_den = [], []
    for kvh in heads:
        pk = [jnp.exp2(t - ms[kvh]) for t in tiles[kvh]]
        st = pk[0]
        for t in pk[1:]:
            st = st + t
        inv_den.append(1.0 / (jnp.exp2(sinks[kvh] - ms[kvh]) + jnp.sum(st, axis=-1, keepdims=True)))
        pcats.append(jnp.concatenate([p.astype(BF16) for p in pk], axis=1))
    acc = jnp.zeros((ATT_GROUP * T, LANES), F32)
    for kvh in heads:
        acc = acc + jnp.dot(pcats[kvh], vcats[kvh], preferred_element_type=F32) * inv_den[kvh]
    for g in range(ATT_GROUP):
        og = acc[g * T:(g + 1) * T, :] * zb_ref[0, :, g * LANES:(g + 1) * LANES].astype(F32)
        o_ref[0, :, g * LANES:(g + 1) * LANES] = og.astype(BF16)


def _attn_call(sink, layer, q, kv, kvc, zb, *, local):
    b, l, _ = q.shape
    nb = l // ATT_BLOCK
    lc = kvc.shape[1]
    blk = lambda f: pl.BlockSpec((1, ATT_BLOCK, 2 * LANES), f)
    return pl.pallas_call(
        functools.partial(_attn_kernel, local=local, layer=layer),
        grid=(b, nb),
        in_specs=[
            pl.BlockSpec(memory_space=pltpu.SMEM),
            pl.BlockSpec((1, ATT_BLOCK, ATT_WIDTH), lambda i, j: (i, j, 0)),
            blk(lambda i, j: (i, jnp.maximum(j - 1, 0), 0)),
            blk(lambda i, j: (i, j, 0)),
            blk(lambda i, j: (i, jnp.minimum(j + 1, nb - 1), 0)),
            pl.BlockSpec((1, lc, 2 * LANES), lambda i, j: (i, 0, 0)),
            pl.BlockSpec((1, ATT_BLOCK, ATT_WIDTH), lambda i, j: (i, j, 0)),
        ],
        out_specs=pl.BlockSpec((1, ATT_BLOCK, ATT_WIDTH), lambda i, j: (i, j, 0)),
        out_shape=jax.ShapeDtypeStruct((b, l, ATT_WIDTH), BF16),
        compiler_params=_cparams(("arbitrary", "arbitrary")),
        name="attn_local" if local else "attn_ctx",
    )(sink, q, kv, kv, kv, kvc, zb)


def _merge_kernel(x_ref, gate_ref, of_ref, ob_ref, za_ref, yb_ref, ga_ref, gb_ref, gnw_ref,
                  wpa_ref, wpb_ref, wout_ref, o_ref):
    gnw = gnw_ref[...]
    ya_parts = []
    for h in range(GDN_HEADS):
        sl = slice(h * LANES, (h + 1) * LANES)
        o = of_ref[0, :, sl].astype(F32) + ob_ref[0, :, sl].astype(F32)
        on = o * lax.rsqrt(jnp.mean(o * o, axis=-1, keepdims=True) + EPS) * gnw
        ya_parts.append((on * za_ref[0, :, sl].astype(F32)).astype(BF16))
    ya = jnp.concatenate(ya_parts, axis=1)
    pa = jnp.dot(ya, wpa_ref[...], preferred_element_type=F32)
    pb = jnp.dot(yb_ref[0], wpb_ref[...], preferred_element_type=F32)
    y = ga_ref[0].astype(F32) * pa + gb_ref[0].astype(F32) * pb
    out = jnp.dot(y.astype(BF16), wout_ref[...], preferred_element_type=F32)
    o_ref[0] = x_ref[0] + gate_ref[0] * out


def _merge_call(x, mod, mod_row, layer, of, ob, za, yb, ga, gb, gnw, wpa, wpb, wout, *, tm):
    b, l, _ = x.shape
    rows = mod.shape[0] // (3 * DEPTH)
    row = lambda w: pl.BlockSpec((1, tm, w), lambda i, j: (i, j, 0))
    return pl.pallas_call(
        _merge_kernel,
        grid=(b, l // tm),
        in_specs=[
            row(D_MODEL), _mod_spec(layer, 2, rows, mod_row),
            row(GDN_WIDTH), row(GDN_WIDTH), row(GDN_WIDTH), row(ATT_WIDTH), row(D_MODEL), row(D_MODEL),
            _layer_spec(layer, 1, LANES),
            _layer_spec(layer, GDN_WIDTH, D_MODEL),
            _layer_spec(layer, ATT_WIDTH, D_MODEL),
            _layer_spec(layer, D_MODEL, D_MODEL),
        ],
        out_specs=row(D_MODEL),
        out_shape=jax.ShapeDtypeStruct((b, l, D_MODEL), F32),
        compiler_params=_cparams(("arbitrary", "arbitrary")),
        name="merge",
    )(x, mod, of, ob, za, yb, ga, gb, gnw, wpa, wpb, wout)


def _pick_tile(l, pref):
    t = min(pref, l)
    while l % t:
        t //= 2
    return t


def kernel(x, c, ctx, c_ctx, norm_w, w_mod, b_mod, w_in, conv_w, a_log, dt_bias, gdn_norm_w,
           q_norm_w, k_norm_w, sink, w_proj_a, w_proj_b, w_out):
    b, l, _ = x.shape
    lc = ctx.shape[1]
    assert l % ATT_BLOCK == 0 and lc % ATT_BLOCK == 0 and x.shape[2] == D_MODEL

    rows = ((b + 1 + 7) // 8) * 8
    c_all = jnp.concatenate([c, c_ctx[None, :], jnp.zeros((rows - b - 1, D_MODEL), F32)], axis=0)
    mod = _mod_call(c_all, w_mod, b_mod).reshape(DEPTH * 3 * rows, 1, D_MODEL)

    cos_l, sin_l = _rope_tables(l)
    cos_c = jnp.ones((lc, LANES), F32)
    sin_c = jnp.zeros((lc, LANES), F32)
    grp = (np.arange(LANES) // ROPE_F) % ATT_KV_HEADS
    gmat = jnp.asarray((grp[:, None] == grp[None, :]).astype(np.float32), dtype=BF16)
    s0 = jnp.zeros((b, 2, GDN_HEADS, LANES, LANES), F32)
    tp_l = _pick_tile(l, 1024)
    tm_l = _pick_tile(l, 512)
    tm_c = _pick_tile(lc, 256)
    tc_l = _pick_tile(l, 512)
    tc_c = _pick_tile(lc, 256)
    tch_l = _pick_tile(l, 1024)

    w_packed_all = _pack_w_in(w_in)
    ng = 2 * GDN_HEADS
    gpad = jnp.zeros((DEPTH, ng), F32)
    gvec_all = jnp.concatenate([
        jnp.concatenate([gpad, jnp.exp(a_log.astype(F32)).reshape(DEPTH, ng),
                         jnp.zeros((DEPTH, LANES - 2 * ng), F32)], axis=1)[:, None, :],
        jnp.concatenate([gpad, dt_bias.astype(F32).reshape(DEPTH, ng),
                         jnp.zeros((DEPTH, LANES - 2 * ng), F32)], axis=1)[:, None, :],
        jnp.zeros((DEPTH, 6, LANES), F32)], axis=1)
    qn_all = (jax.vmap(_norm_lanes)(q_norm_w) * (ATT_HD ** -0.5 * LOG2E))[:, None, :]
    kn_all = jax.vmap(_norm_lanes)(k_norm_w)[:, None, :]
    cw_all = jnp.pad(conv_w, ((0, 0), (0, 8 - SHORT_CONV), (0, 0)))
    normw_all = norm_w[:, None, :]
    gnw_all = gdn_norm_w[:, None, :]
    wpa_all = w_proj_a.astype(BF16)
    wpb_all = jnp.swapaxes(w_proj_b.reshape(DEPTH, ATT_KV_HEADS, ATT_GROUP, ATT_HD, D_MODEL), 1, 2).reshape(
        DEPTH, ATT_WIDTH, D_MODEL).astype(BF16)
    wout_all = w_out.astype(BF16)
    proj_params = (normw_all, w_packed_all, gvec_all, qn_all, kn_all, gmat)
    merge_params = (gnw_all, wpa_all, wpb_all, wout_all)

    for i in range(DEPTH):
        update_ctx = i < DEPTH - 1

        outs_c = _proj_call(ctx, mod, b, i, *proj_params, cos_c, sin_c, with_rest=update_ctx, tm=tm_c)
        qkv_c, gates_c, kv_c = outs_c[:3]
        prep_c = _prep_call(qkv_c, cw_all, i, gates_c, tc=tc_c)
        of_c, ob_c, s_ctx = _chain_call(*prep_c, s0, tc=tc_c)

        qkv_l, gates_l, kv_l, za_l, q_l, zb_l, ga_l, gb_l = _proj_call(
            x, mod, None, i, *proj_params, cos_l, sin_l, with_rest=True, tm=tp_l)
        prep_l = _prep_call(qkv_l, cw_all, i, gates_l, tc=tc_l)
        of_l, ob_l, _ = _chain_call(*prep_l, s_ctx, tc=tch_l)
        yb_l = _attn_call(sink, i, q_l, kv_l, kv_c, zb_l, local=True)
        x_new = _merge_call(x, mod, None, i, of_l, ob_l, za_l, yb_l, ga_l, gb_l, *merge_params, tm=tm_l)

        if update_ctx:
            za_c, q_c, zb_c, ga_c, gb_c = outs_c[3:]
            yb_c = _attn_call(sink, i, q_c, kv_c, kv_c, zb_c, local=False)
            ctx = _merge_call(ctx, mod, b, i, of_c, ob_c, za_c, yb_c, ga_c, gb_c, *merge_params, tm=tm_c)
        x = x_new
    return x
```

```python
import functools
import math

import numpy as np
import jax
import jax.numpy as jnp
from jax import lax
from jax.experimental import pallas as pl
from jax.experimental.pallas import tpu as pltpu

F32 = jnp.float32
BF16 = jnp.bfloat16

D_MODEL = 1024
DEPTH = 2
GRID_W = 64
EPS = 1e-6
NEG_INF = -1e30

GDN_HEADS = 4
GDN_DK = 128
GDN_DV = 128
GDN_QKV = GDN_HEADS * (2 * GDN_DK + GDN_DV)
GDN_WIDTH = GDN_HEADS * GDN_DV
SHORT_CONV = 5
CHUNK = 64

ATT_HEADS = 8
ATT_KV_HEADS = 2
ATT_GROUP = ATT_HEADS // ATT_KV_HEADS
ATT_HD = 64
ATT_WIDTH = ATT_HEADS * ATT_HD
ATT_KV_WIDTH = ATT_KV_HEADS * ATT_HD
ATT_BLOCK = 128
ROPE_BASE = 10000.0
AXIS_DIM = ATT_HD // 2
ROPE_F = AXIS_DIM // 2
LOG2E = math.log2(math.e)

N_STATE = GDN_QKV + 4 * GDN_HEADS + 2 * ATT_KV_WIDTH

LANES = 128
P_QKV = 0
P_GATE = GDN_QKV
P_K = P_GATE + LANES
P_V = P_K + ATT_KV_WIDTH
P_STATE_END = P_V + ATT_KV_WIDTH
P_ZA = P_STATE_END
P_Q = P_ZA + GDN_WIDTH
P_ZB = P_Q + ATT_WIDTH
P_GA = P_ZB + ATT_WIDTH
P_GB = P_GA + D_MODEL
P_END = P_GB + D_MODEL

VMEM_LIMIT = 56 * 1024 * 1024


def _cparams(sem):
    return pltpu.CompilerParams(dimension_semantics=sem, vmem_limit_bytes=VMEM_LIMIT)


def _k_cols(w):
    lead = w.shape[:-1]
    w = w.reshape(lead + (ATT_KV_HEADS, 4, ROPE_F))
    return jnp.swapaxes(w, -3, -2).reshape(lead + (ATT_KV_WIDTH,))


def _q_cols(w):
    lead = w.shape[:-1]
    w = w.reshape(lead + (ATT_KV_HEADS, ATT_GROUP, 4, ROPE_F))
    return jnp.moveaxis(w, -4, -2).reshape(lead + (ATT_WIDTH,))


def _o_cols(w):
    lead = w.shape[:-1]
    w = w.reshape(lead + (ATT_KV_HEADS, ATT_GROUP, ATT_HD))
    return jnp.swapaxes(w, -3, -2).reshape(lead + (ATT_WIDTH,))


def _norm_lanes(w):
    w = w.reshape(4, 1, ROPE_F)
    return jnp.broadcast_to(w, (4, ATT_KV_HEADS, ROPE_F)).reshape(LANES)


def _pack_w_in(w_in):
    w_in = w_in.astype(BF16)
    o = 0
    qkv = w_in[..., o:o + GDN_QKV]; o += GDN_QKV
    gates = w_in[..., o:o + 4 * GDN_HEADS]; o += 4 * GDN_HEADS
    kb = w_in[..., o:o + ATT_KV_WIDTH]; o += ATT_KV_WIDTH
    vb = w_in[..., o:o + ATT_KV_WIDTH]; o += ATT_KV_WIDTH
    za = w_in[..., o:o + GDN_WIDTH]; o += GDN_WIDTH
    qb = w_in[..., o:o + ATT_WIDTH]; o += ATT_WIDTH
    zb = w_in[..., o:o + ATT_WIDTH]; o += ATT_WIDTH
    ga = w_in[..., o:o + D_MODEL]; o += D_MODEL
    gb = w_in[..., o:o + D_MODEL]
    gates = jnp.concatenate([gates, jnp.zeros(gates.shape[:-1] + (LANES - 4 * GDN_HEADS,), BF16)], axis=-1)
    return jnp.concatenate(
        [qkv, gates, _k_cols(kb), vb, za, _q_cols(qb), _o_cols(zb), ga, gb], axis=-1)


def _rope_tables(seq_len):
    t = jnp.arange(seq_len, dtype=F32)
    pos_row = jnp.floor(t / GRID_W)
    pos_col = t - pos_row * GRID_W
    inv = ROPE_BASE ** (-jnp.arange(0, AXIS_DIM, 2, dtype=F32) / AXIS_DIM)
    ang_r = pos_row[:, None] * inv[None, :]
    ang_c = pos_col[:, None] * inv[None, :]

    def lay(a):
        return jnp.concatenate([a, a], axis=1)

    cos = jnp.concatenate([lay(jnp.cos(ang_r)), lay(jnp.cos(ang_r)), lay(jnp.cos(ang_c)), lay(jnp.cos(ang_c))], axis=1)
    sin = jnp.concatenate([-lay(jnp.sin(ang_r)), lay(jnp.sin(ang_r)), -lay(jnp.sin(ang_c)), lay(jnp.sin(ang_c))], axis=1)
    return cos, sin


def _mod_kernel(c_ref, w_ref, b_ref, o_ref):
    c = c_ref[...]
    s = c * jax.nn.sigmoid(c)
    o_ref[0, 0] = jnp.dot(s.astype(BF16), w_ref[0].astype(BF16), preferred_element_type=F32) + b_ref[0]


def _mod_call(c_all, w_mod, b_mod):
    rows = c_all.shape[0]
    nblk = 3
    return pl.pallas_call(
        _mod_kernel,
        grid=(DEPTH, nblk),
        in_specs=[
            pl.BlockSpec((rows, D_MODEL), lambda l, j: (0, 0)),
            pl.BlockSpec((1, D_MODEL, D_MODEL), lambda l, j: (l, 0, j)),
            pl.BlockSpec((1, 1, D_MODEL), lambda l, j: (l, 0, j)),
        ],
        out_specs=pl.BlockSpec((1, 1, rows, D_MODEL), lambda l, j: (l, j, 0, 0)),
        out_shape=jax.ShapeDtypeStruct((DEPTH, nblk, rows, D_MODEL), F32),
        compiler_params=_cparams(("arbitrary", "arbitrary")),
        name="mod",
    )(c_all, w_mod, b_mod.reshape(DEPTH, 1, 3 * D_MODEL))


def _mod_spec(layer, kind, rows, row):
    base = (layer * 3 + kind) * rows
    if row is None:
        return pl.BlockSpec((1, 1, D_MODEL), lambda i, j: (base + i, 0, 0))
    return pl.BlockSpec((1, 1, D_MODEL), lambda i, j: (base + row, 0, 0))


def _layer_spec(layer, *block):
    zeros = (0,) * len(block)
    return pl.BlockSpec((None,) + tuple(block), lambda i, j: (layer,) + zeros)


def _swap32(t):
    lane = lax.broadcasted_iota(jnp.int32, t.shape, 1)
    even = (lane // 32) % 2 == 0
    return jnp.where(even, pltpu.roll(t, 96, 1), pltpu.roll(t, 32, 1))


def _headnorm_rope(parts, nws, gmat, cos, sin):
    t = jnp.concatenate(parts, axis=0) if len(parts) > 1 else parts[0]
    ss = jnp.dot((t * t).astype(BF16), gmat, preferred_element_type=F32)
    inv = lax.rsqrt(ss * (1.0 / ATT_HD) + EPS)
    tm = parts[0].shape[0]
    outs = []
    for i, nw in enumerate(nws):
        tn = parts[i] * inv[i * tm:(i + 1) * tm] * nw
        outs.append(tn * cos + _swap32(tn) * sin)
    return outs


def _softplus(x):
    return jnp.maximum(x, 0.0) + jnp.log1p(jnp.exp(-jnp.abs(x)))


def _proj_kernel(x_ref, scale_ref, shift_ref, normw_ref, w_ref, gvec_ref, qn_ref, kn_ref, gmat_ref,
                 cos_ref, sin_ref, *out_refs, with_rest):
    if with_rest:
        qkv_ref, gates_ref, kv_ref, za_ref, q_ref, zb_ref, ga_ref, gb_ref = out_refs
    else:
        qkv_ref, gates_ref, kv_ref = out_refs
    x = x_ref[0]
    ms = jnp.mean(x * x, axis=-1, keepdims=True)
    a = normw_ref[...] * (1.0 + scale_ref[0])
    h = (x * lax.rsqrt(ms + EPS) * a + shift_ref[0]).astype(BF16)

    def proj(lo, width):
        return jnp.dot(h, w_ref[:, lo:lo + width], preferred_element_type=F32)

    for c in range(GDN_QKV // 512):
        qkv_ref[0, :, c * 512:(c + 1) * 512] = proj(P_QKV + c * 512, 512).astype(BF16)

    g = proj(P_GATE, LANES)
    lane = lax.broadcasted_iota(jnp.int32, g.shape, 1)
    beta = jax.nn.sigmoid(g)
    dec = -gvec_ref[0:1, :] * _softplus(g + gvec_ref[1:2, :])
    gates_ref[0] = jnp.where(lane < 2 * GDN_HEADS, beta, dec)

    gmat = gmat_ref[...]
    cos = cos_ref[...]
    sin = sin_ref[...]
    kv_ref[0, :, LANES:2 * LANES] = proj(P_V, LANES).astype(BF16)
    parts = [proj(P_K, LANES)]
    nws = [kn_ref[...]]
    if with_rest:
        qraw = proj(P_Q, ATT_WIDTH)
        parts += [qraw[:, gi * LANES:(gi + 1) * LANES] for gi in range(ATT_GROUP)]
        nws += [qn_ref[...]] * ATT_GROUP
    normed = _headnorm_rope(parts, nws, gmat, cos, sin)
    kv_ref[0, :, 0:LANES] = normed[0].astype(BF16)

    if with_rest:
        for gi in range(ATT_GROUP):
            q_ref[0, :, gi * LANES:(gi + 1) * LANES] = normed[1 + gi].astype(BF16)
        z = proj(P_ZA, GDN_WIDTH)
        za_ref[0] = (z * jax.nn.sigmoid(z)).astype(BF16)
        z = proj(P_ZB, ATT_WIDTH)
        zb_ref[0] = (z * jax.nn.sigmoid(z)).astype(BF16)
        for c in range(D_MODEL // 512):
            ga_ref[0, :, c * 512:(c + 1) * 512] = jax.nn.sigmoid(proj(P_GA + c * 512, 512)).astype(BF16)
            gb_ref[0, :, c * 512:(c + 1) * 512] = jax.nn.sigmoid(proj(P_GB + c * 512, 512)).astype(BF16)


def _proj_call(x, mod, mod_row, layer, normw, w_packed, gvec, qn, kn, gmat, cos, sin, *, with_rest, tm):
    b, l, _ = x.shape
    nw = P_END if with_rest else P_STATE_END
    rows = mod.shape[0] // (3 * DEPTH)
    const2 = lambda i, j: (0, 0)
    row_map = lambda i, j: (i, j, 0)
    in_specs = [
        pl.BlockSpec((1, tm, D_MODEL), row_map),
        _mod_spec(layer, 1, rows, mod_row),
        _mod_spec(layer, 0, rows, mod_row),
        _layer_spec(layer, 1, D_MODEL),
        pl.BlockSpec((None, D_MODEL, nw), lambda i, j: (layer, 0, 0), pipeline_mode=pl.Buffered(1)),
        _layer_spec(layer, 8, LANES),
        _layer_spec(layer, 1, LANES),
        _layer_spec(layer, 1, LANES),
        pl.BlockSpec((LANES, LANES), const2),
        pl.BlockSpec((tm, LANES), lambda i, j: (j, 0)),
        pl.BlockSpec((tm, LANES), lambda i, j: (j, 0)),
    ]
    widths = [(GDN_QKV, BF16), (LANES, F32), (2 * LANES, BF16)]
    if with_rest:
        widths += [(GDN_WIDTH, BF16), (ATT_WIDTH, BF16), (ATT_WIDTH, BF16), (D_MODEL, BF16), (D_MODEL, BF16)]
    out_specs = [pl.BlockSpec((1, tm, w), row_map) for w, _ in widths]
    out_shape = [jax.ShapeDtypeStruct((b, l, w), dt) for w, dt in widths]
    return pl.pallas_call(
        functools.partial(_proj_kernel, with_rest=with_rest),
        grid=(b, l // tm),
        in_specs=in_specs,
        out_specs=out_specs,
        out_shape=out_shape,
        compiler_params=_cparams(("arbitrary", "arbitrary")),
        name="proj_full" if with_rest else "proj_state",
    )(x, mod, mod, normw, w_packed, gvec, qn, kn, gmat, cos, sin)


HALO = 16


def _blockdiag(y, isf):
    zero = jnp.zeros_like(y)
    return jnp.concatenate([jnp.where(isf, y, zero), jnp.where(isf, zero, y)], axis=0)


def _short_conv_stage(raw_ref, prev_ref, next_ref, cw_ref, raw_scr, qkv_scr, tc):
    j = pl.program_id(1)
    n = pl.num_programs(1)
    half = SHORT_CONV // 2
    for cb in range(GDN_QKV // LANES):
        sl = slice(cb * LANES, (cb + 1) * LANES)
        raw_scr[cb, 0:HALO, :] = jnp.where(j > 0, prev_ref[0, :, sl].astype(F32), 0.0)
        raw_scr[cb, HALO:HALO + tc, :] = raw_ref[0, :, sl].astype(F32)
        raw_scr[cb, HALO + tc:2 * HALO + tc, :] = jnp.where(j < n - 1, next_ref[0, :, sl].astype(F32), 0.0)
        y = raw_scr[cb, HALO - half:HALO - half + tc, :] * cw_ref[0:1, sl]
        for tap in range(1, SHORT_CONV):
            y = y + raw_scr[cb, HALO - half + tap:HALO - half + tap + tc, :] * cw_ref[tap:tap + 1, sl]
        y = y * jax.nn.sigmoid(y)
        if cb < 2 * GDN_HEADS:
            scale = GDN_DK ** -0.5 if cb < GDN_HEADS else 1.0
            y = y * (lax.rsqrt(jnp.sum(y * y, axis=-1, keepdims=True) + EPS) * scale)
        qkv_scr[:, sl] = y.astype(BF16)


def _prep_kernel(raw_ref, prev_ref, next_ref, cw_ref, gates_ref, u_ref, w_ref, kdt_ref, qd_ref, qk_ref, eg_ref,
                 raw_scr, qkv_ref, a_scr, t_scr, rhs_scr, *, tc):
    H = GDN_HEADS
    C = CHUNK

    @pl.when(jnp.logical_and(pl.program_id(0) == 0, pl.program_id(1) == 0))
    def _():
        rhs_scr[...] = jnp.zeros(rhs_scr.shape, BF16)

    _short_conv_stage(raw_ref, prev_ref, next_ref, cw_ref, raw_scr, qkv_ref, tc)
    G = gates_ref[0]
    gT = G.T[2 * H:4 * H, :]
    lane_t = lax.broadcasted_iota(jnp.int32, gT.shape, 1)
    pos = lane_t % C
    pre = gT
    suf = gT
    for s in (1, 2, 4, 8, 16, 32):
        pre = pre + jnp.where(pos >= s, pltpu.roll(pre, s, 1), 0.0)
        suf = suf + jnp.where(pos < C - s, pltpu.roll(suf, tc - s, 1), 0.0)
    row8 = lax.broadcasted_iota(jnp.int32, gT.shape, 0)
    gcT = jnp.where(row8 < H, pre, suf)
    gc = jnp.concatenate([gcT, jnp.zeros((LANES - 2 * H, tc), F32)], axis=0).T

    lane = lax.broadcasted_iota(jnp.int32, (C, LANES), 1)
    ii = lax.broadcasted_iota(jnp.int32, (C, LANES), 0)
    jj = lane % C
    isf = lane < C
    isb = jnp.logical_not(isf)
    incl = jnp.logical_or(jnp.logical_and(isf, ii >= jj), jnp.logical_and(isb, ii <= jj))
    strict = jnp.logical_or(jnp.logical_and(isf, ii > jj), jnp.logical_and(isb, ii < jj))
    eye2 = (ii == jj).astype(F32)
    same = {kk: (ii // kk) == (jj // kk) for kk in (2, 4, 8, 16, 32, 64)}
    lane8 = lax.broadcasted_iota(jnp.int32, (1, LANES), 1)

    def colb(arr, r0, c):
        return jnp.broadcast_to(arr[r0:r0 + C, c:c + 1], (C, LANES))

    for s in range(tc // C):
        r0 = s * C
        glrow = jnp.where(lane8 < H, gc[r0 + C - 1:r0 + C, :], gc[r0:r0 + 1, :])
        eg_ref[0, s] = jnp.broadcast_to(jnp.exp(glrow), (8, LANES))

    items = [(lt, 2 * p + hp, halfsel) for lt in range(tc // LANES) for p in range(H // 2)
             for halfsel in range(2) for hp in range(2)]
    gts = {}
    for lt in range(tc // LANES):
        gt = gcT[:, lt * LANES:(lt + 1) * LANES]
        gts[lt] = (gt, pltpu.roll(gt, C, 1))

    kd_parts = {}
    qk_hold = {}

    def stage1(idx):
        lt, h, halfsel = items[idx]
        gt, gt_r = gts[lt]
        r0 = (2 * lt + halfsel) * C
        if halfsel == 0:
            row_f, row_b = gt[h:h + 1, :], gt_r[H + h:H + h + 1, :]
        else:
            row_f, row_b = gt_r[h:h + 1, :], gt[H + h:H + h + 1, :]
        gc_row2 = jnp.broadcast_to(jnp.where(lane8 < C, row_f, row_b), (C, LANES))
        gcf = colb(gc, r0, h)
        gcb = colb(gc, r0, H + h)
        dec = jnp.where(incl, jnp.exp(jnp.where(incl, jnp.where(isf, gcf, gcb) - gc_row2, 0.0)), 0.0)
        bf = colb(G, r0, h)
        bb = colb(G, r0, H + h)
        qt = qkv_ref[r0:r0 + C, h * LANES:(h + 1) * LANES]
        kt = qkv_ref[r0:r0 + C, (H + h) * LANES:(H + h + 1) * LANES]
        vt = qkv_ref[r0:r0 + C, (2 * H + h) * LANES:(2 * H + h + 1) * LANES]
        kq = lax.dot_general(jnp.concatenate([kt, qt], axis=0), jnp.concatenate([kt, kt], axis=0),
                             (((1,), (1,)), ((), ())), preferred_element_type=F32)
        a2 = jnp.where(strict, jnp.where(isf, bf, bb) * kq[0:C] * dec, 0.0)
        a_scr[idx] = a2
        t_scr[idx] = eye2 - jnp.where(same[2], a2, 0.0)
        qk2 = kq[C:2 * C] * dec
        if h % 2 == 0:
            qk_hold[(lt, h, halfsel)] = qk2
        else:
            qk_even = qk_hold.pop((lt, h - 1, halfsel))
            pc = (h // 2) * LANES
            qk_ref[0, 0, r0:r0 + C, pc:pc + LANES] = jnp.where(isf, qk_even, pltpu.roll(qk2, C, 1)).astype(BF16)
            qk_ref[0, 1, r0:r0 + C, pc:pc + LANES] = jnp.where(isf, pltpu.roll(qk_even, C, 1), qk2).astype(BF16)

        kf = kt.astype(F32)
        vf = vt.astype(F32)
        qf = qt.astype(F32)
        egf = jnp.exp(gcf)
        egb = jnp.exp(gcb)
        rhs_scr[idx, 0:C, 0:2 * LANES] = jnp.concatenate([vf * bf, kf * (bf * egf)], axis=1).astype(BF16)
        rhs_scr[idx, C:2 * C, 2 * LANES:4 * LANES] = jnp.concatenate([vf * bb, kf * (bb * egb)], axis=1).astype(BF16)
        cf = h * LANES
        qd_ref[0, 0, r0:r0 + C, cf:cf + LANES] = (qf * egf).astype(BF16)
        qd_ref[0, 1, r0:r0 + C, cf:cf + LANES] = (qf * egb).astype(BF16)
        glf = jnp.broadcast_to(gc[r0 + C - 1:r0 + C, h:h + 1], (C, LANES))
        glb = jnp.broadcast_to(gc[r0:r0 + 1, H + h:H + h + 1], (C, LANES))
        kd_parts[(lt, h, halfsel)] = (kf * jnp.exp(glf - gcf), kf * jnp.exp(glb - gcb))
        if halfsel == 1:
            for d in range(2):
                kdt = jnp.concatenate([kd_parts[(lt, h, 0)][d], kd_parts[(lt, h, 1)][d]], axis=0).T
                kdt_ref[0, 0, (d * H + h) * LANES:(d * H + h + 1) * LANES, lt * LANES:(lt + 1) * LANES] = kdt.astype(BF16)

    levels = (2, 4, 8, 16, 32)

    def round_a(kk, idxs):
        emask = jnp.logical_and(same[2 * kk], jnp.logical_not(same[kk]))
        ps = []
        for idx in idxs:
            e = jnp.where(emask, a_scr[idx], 0.0).astype(BF16)
            ps.append(jnp.dot(e, _blockdiag(t_scr[idx].astype(BF16), isf), preferred_element_type=F32))
        return ps

    def round_b(idxs, ps):
        for idx, p in zip(idxs, ps):
            t2 = t_scr[idx]
            t_scr[idx] = t2 - jnp.dot(t2.astype(BF16), _blockdiag(p.astype(BF16), isf),
                                      preferred_element_type=F32)

    def stage3(idx):
        lt, h, halfsel = items[idx]
        r0 = (2 * lt + halfsel) * C
        uw = jnp.dot(t_scr[idx].astype(BF16), rhs_scr[idx], preferred_element_type=F32)
        cf = h * LANES
        u_ref[0, 0, r0:r0 + C, cf:cf + LANES] = uw[:, 0:LANES].astype(BF16)
        w_ref[0, 0, r0:r0 + C, cf:cf + LANES] = uw[:, LANES:2 * LANES].astype(BF16)
        u_ref[0, 1, r0:r0 + C, cf:cf + LANES] = uw[:, 2 * LANES:3 * LANES].astype(BF16)
        w_ref[0, 1, r0:r0 + C, cf:cf + LANES] = uw[:, 3 * LANES:4 * LANES].astype(BF16)

    everything = list(range(len(items)))
    for idx in everything:
        stage1(idx)
    for kk in levels:
        round_b(everything, round_a(kk, everything))
    for idx in everything:
        stage3(idx)


def _prep_call(qkv_raw, conv_w, layer, gates, *, tc):
    b, l, _ = qkv_raw.shape
    row_map = lambda i, j: (i, j, 0)
    wide = 2 * GDN_HEADS * LANES
    hw = GDN_HEADS * LANES
    dir_map = lambda i, j: (i, 0, j, 0)
    n_items = (tc // CHUNK) * GDN_HEADS
    hpt = tc // HALO
    nhalo = l // HALO
    n = l // tc
    return pl.pallas_call(
        functools.partial(_prep_kernel, tc=tc),
        grid=(b, l // tc),
        in_specs=[
            pl.BlockSpec((1, tc, GDN_QKV), row_map),
            pl.BlockSpec((1, HALO, GDN_QKV), lambda i, j: (i, jnp.maximum(j * hpt - 1, 0), 0)),
            pl.BlockSpec((1, HALO, GDN_QKV), lambda i, j: (i, jnp.minimum((j + 1) * hpt, nhalo - 1), 0)),
            _layer_spec(layer, 8, GDN_QKV),
            pl.BlockSpec((1, tc, LANES), row_map),
        ],
        out_specs=[
            pl.BlockSpec((1, 2, tc, hw), dir_map),
            pl.BlockSpec((1, 2, tc, hw), dir_map),
            pl.BlockSpec((1, 1, wide, tc), lambda i, j: (i, j, 0, 0)),
            pl.BlockSpec((1, 2, tc, hw), dir_map),
            pl.BlockSpec((1, 2, tc, GDN_HEADS * CHUNK), dir_map),
            pl.BlockSpec((1, tc // CHUNK, 8, LANES), lambda i, j: (i, j, 0, 0)),
        ],
        out_shape=[
            jax.ShapeDtypeStruct((b, 2, l, hw), BF16),
            jax.ShapeDtypeStruct((b, 2, l, hw), BF16),
            jax.ShapeDtypeStruct((b, l // tc, wide, tc), BF16),
            jax.ShapeDtypeStruct((b, 2, l, hw), BF16),
            jax.ShapeDtypeStruct((b, 2, l, GDN_HEADS * CHUNK), BF16),
            jax.ShapeDtypeStruct((b, l // CHUNK, 8, LANES), F32),
        ],
        scratch_shapes=[
            pltpu.VMEM((GDN_QKV // LANES, tc + 2 * HALO, LANES), F32),
            pltpu.VMEM((tc, GDN_QKV), BF16),
            pltpu.VMEM((n_items, CHUNK, LANES), F32),
            pltpu.VMEM((n_items, CHUNK, LANES), F32),
            pltpu.VMEM((n_items, 2 * CHUNK, 4 * LANES), BF16),
        ],
        compiler_params=_cparams(("arbitrary", "arbitrary")),
        name="gdn_prep",
    )(qkv_raw, qkv_raw, qkv_raw, conv_w, gates)


def _chain_kernel(uf_ref, wf_ref, kdtf_ref, qdf_ref, qkf_ref, egf_ref,
                  ub_ref, wb_ref, kdtb_ref, qdb_ref, qkb_ref, egb_ref, s0_ref,
                  of_ref, ob_ref, sfin_ref, s_scr, *, tc, tcp):
    H = GDN_HEADS
    C = CHUNK
    i = pl.program_id(1)
    n = pl.num_programs(1)

    @pl.when(i == 0)
    def _():
        s_scr[...] = s0_ref[0]

    nch = tc // C
    zc = jnp.zeros((C, LANES), BF16)
    refs = ((uf_ref, wf_ref, kdtf_ref, qdf_ref, qkf_ref, egf_ref, of_ref),
            (ub_ref, wb_ref, kdtb_ref, qdb_ref, qkb_ref, egb_ref, ob_ref))
    chains = [(d, h) for d in range(2) for h in range(H)]
    for c in range(nch):
        rs = []
        for d, h in chains:
            u_ref, w_ref, kdt_ref, qd_ref, qk_ref, eg_ref, o_ref = refs[d]
            r0 = (c if d == 0 else nch - 1 - c) * C
            cs = h * LANES
            lhs = jnp.concatenate([w_ref[0, 0, r0:r0 + C, cs:cs + LANES], qd_ref[0, 0, r0:r0 + C, cs:cs + LANES]],
                                  axis=0)
            rs.append(jnp.dot(lhs, s_scr[d, h].astype(BF16), preferred_element_type=F32))
        vbs = []
        for (d, h), r in zip(chains, rs):
            u_ref, w_ref, kdt_ref, qd_ref, qk_ref, eg_ref, o_ref = refs[d]
            r0 = (c if d == 0 else nch - 1 - c) * C
            cs = h * LANES
            vbs.append((u_ref[0, 0, r0:r0 + C, cs:cs + LANES].astype(F32) - r[0:C]).astype(BF16))
        for (d, h), vb in zip(chains, vbs):
            u_ref, w_ref, kdt_ref, qd_ref, qk_ref, eg_ref, o_ref = refs[d]
            cc = c if d == 0 else nch - 1 - c
            pt, within = divmod(cc * C, tcp)
            lt, par = divmod(within // C, 2)
            cs = h * LANES
            v_par = jnp.concatenate([vb, zc], axis=0) if par == 0 else jnp.concatenate([zc, vb], axis=0)
            kdt = kdt_ref[0, pt, cs:cs + LANES, lt * LANES:(lt + 1) * LANES]
            eg = jnp.broadcast_to(eg_ref[0, cc, 0:1, d * H + h:d * H + h + 1], (LANES, LANES))
            s_scr[d, h] = s_scr[d, h] * eg + jnp.dot(kdt, v_par, preferred_element_type=F32)
        for d in range(2):
            u_ref, w_ref, kdt_ref, qd_ref, qk_ref, eg_ref, o_ref = refs[d]
            r0 = (c if d == 0 else nch - 1 - c) * C
            for p in range(H // 2):
                v0, v1 = vbs[d * H + 2 * p], vbs[d * H + 2 * p + 1]
                v_pair = jnp.concatenate([jnp.concatenate([v0, zc], axis=1), jnp.concatenate([zc, v1], axis=1)], axis=0)
                intra = jnp.dot(qk_ref[0, 0, r0:r0 + C, p * LANES:(p + 1) * LANES], v_pair,
                                preferred_element_type=F32)
                for hp in range(2):
                    h = 2 * p + hp
                    o = rs[d * H + h][C:2 * C] + intra[:, hp * LANES:(hp + 1) * LANES]
                    o_ref[0, r0:r0 + C, h * LANES:(h + 1) * LANES] = o.astype(BF16)

    @pl.when(i == n - 1)
    def _():
        sfin_ref[0] = s_scr[...]


def _chain_call(u, w, kdt, qd, qk, eg, s0, *, tc):
    b, _, l, hw = u.shape
    tcp = kdt.shape[3]
    assert tc % tcp == 0
    tpr = tc // tcp
    n = l // tc
    fwd = lambda i, j: (i, 0, j, 0)
    bwd = lambda i, j: (i, 1, n - 1 - j, 0)
    fwd3 = lambda i, j: (i, j, 0)
    bwd3 = lambda i, j: (i, n - 1 - j, 0)
    qkw = GDN_HEADS * CHUNK
    nchunk = tc // CHUNK
    in_specs = [
        pl.BlockSpec((1, 1, tc, hw), fwd), pl.BlockSpec((1, 1, tc, hw), fwd),
        pl.BlockSpec((1, tpr, hw, tcp), lambda i, j: (i, j, 0, 0)),
        pl.BlockSpec((1, 1, tc, hw), fwd), pl.BlockSpec((1, 1, tc, qkw), fwd),
        pl.BlockSpec((1, nchunk, 8, LANES), lambda i, j: (i, j, 0, 0)),
        pl.BlockSpec((1, 1, tc, hw), bwd), pl.BlockSpec((1, 1, tc, hw), bwd),
        pl.BlockSpec((1, tpr, hw, tcp), lambda i, j: (i, n - 1 - j, 1, 0)),
        pl.BlockSpec((1, 1, tc, hw), bwd), pl.BlockSpec((1, 1, tc, qkw), bwd),
        pl.BlockSpec((1, nchunk, 8, LANES), lambda i, j: (i, n - 1 - j, 0, 0)),
        pl.BlockSpec((1, 2, GDN_HEADS, LANES, LANES), lambda i, j: (i, 0, 0, 0, 0)),
    ]
    out_specs = [
        pl.BlockSpec((1, tc, hw), fwd3),
        pl.BlockSpec((1, tc, hw), bwd3),
        pl.BlockSpec((1, 2, GDN_HEADS, LANES, LANES), lambda i, j: (i, 0, 0, 0, 0)),
    ]
    out_shape = [
        jax.ShapeDtypeStruct((b, l, hw), BF16),
        jax.ShapeDtypeStruct((b, l, hw), BF16),
        jax.ShapeDtypeStruct((b, 2, GDN_HEADS, LANES, LANES), F32),
    ]
    return pl.pallas_call(
        functools.partial(_chain_kernel, tc=tc, tcp=tcp),
        grid=(b, n),
        in_specs=in_specs,
        out_specs=out_specs,
        out_shape=out_shape,
        scratch_shapes=[pltpu.VMEM((2, GDN_HEADS, LANES, LANES), F32)],
        compiler_params=_cparams(("arbitrary", "arbitrary")),
        name="gdn_chain",
    )(u, w, kdt, qd, qk, eg, u, w, kdt, qd, qk, eg, s0)


def _attn_kernel(sink_ref, q_ref, kvp_ref, kvo_ref, kvn_ref, kvc_ref, zb_ref, o_ref, *, local, layer):
    nb = pl.num_programs(1)
    n = pl.program_id(1)
    T = ATT_BLOCK
    q = q_ref[0]
    qs = jnp.concatenate([q[:, g * LANES:(g + 1) * LANES] for g in range(ATT_GROUP)], axis=0)
    lane = lax.broadcasted_iota(jnp.int32, (1, LANES), 1)
    rowi = lax.broadcasted_iota(jnp.int32, (ATT_GROUP * T, T), 0) % T
    colj = lax.broadcasted_iota(jnp.int32, (ATT_GROUP * T, T), 1)
    grp = lax.broadcasted_iota(jnp.int32, (ATT_GROUP * T, 1), 0) // T

    pieces = [(kvc_ref, None)]
    if local:
        pieces += [(kvp_ref, jnp.logical_and(colj >= rowi, n > 0)),
                   (kvo_ref, None),
                   (kvn_ref, jnp.logical_and(colj <= rowi, n < nb - 1))]

    heads = range(ATT_KV_HEADS)
    vcats, tiles = [], []
    for kvh in heads:
        kmask = ((lane // ROPE_F) % ATT_KV_HEADS == kvh)
        vmask = (lane // ATT_HD == kvh)
        k_rows, v_rows = [], []
        for ref, _ in pieces:
            kv = ref[0]
            zero = jnp.zeros_like(kv[:, 0:LANES])
            k_rows.append(jnp.where(kmask, kv[:, 0:LANES], zero))
            v_rows.append(jnp.where(vmask, kv[:, LANES:2 * LANES], zero))
        vcats.append(jnp.concatenate(v_rows, axis=0))
        s_h = lax.dot_general(qs, jnp.concatenate(k_rows, axis=0), (((1,), (1,)), ((), ())),
                              preferred_element_type=F32)
        th, col = [], 0
        for ref, valid in pieces:
            for c in range(ref.shape[1] // LANES):
                t = s_h[:, col:col + LANES]
                th.append(t if valid is None else jnp.where(valid, t, NEG_INF))
                col += LANES
        tiles.append(th)
    sinks, ms = [], []
    for kvh in heads:
        sink = jnp.zeros((ATT_GROUP * T, 1), F32)
        for g in range(ATT_GROUP):
            sink = jnp.where(grp == g, sink_ref[layer, kvh * ATT_GROUP + g] * LOG2E, sink)
        sinks.append(sink)
        mt = tiles[kvh][0]
        for t in tiles[kvh][1:]:
            mt = jnp.maximum(mt, t)
        ms.append(jnp.maximum(sink, jnp.max(mt, axis=-1, keepdims=True)))
    pcats, inv_den = [], []
    for kvh in heads:
        pk = [jnp.exp2(t - ms[kvh]) for t in tiles[kvh]]
        st = pk[0]
        for t in pk[1:]:
            st = st + t
        inv_den.append(1.0 / (jnp.exp2(sinks[kvh] - ms[kvh]) + jnp.sum(st, axis=-1, keepdims=True)))
        pcats.append(jnp.concatenate([p.astype(BF16) for p in pk], axis=1))
    acc = jnp.zeros((ATT_GROUP * T, LANES), F32)
    for kvh in heads:
        acc = acc + jnp.dot(pcats[kvh], vcats[kvh], preferred_element_type=F32) * inv_den[kvh]
    for g in range(ATT_GROUP):
        og = acc[g * T:(g + 1) * T, :] * zb_ref[0, :, g * LANES:(g + 1) * LANES].astype(F32)
        o_ref[0, :, g * LANES:(g + 1) * LANES] = og.astype(BF16)


def _attn_call(sink, layer, q, kv, kvc, zb, *, local):
    b, l, _ = q.shape
    nb = l // ATT_BLOCK
    lc = kvc.shape[1]
    blk = lambda f: pl.BlockSpec((1, ATT_BLOCK, 2 * LANES), f)
    return pl.pallas_call(
        functools.partial(_attn_kernel, local=local, layer=layer),
        grid=(b, nb),
        in_specs=[
            pl.BlockSpec(memory_space=pltpu.SMEM),
            pl.BlockSpec((1, ATT_BLOCK, ATT_WIDTH), lambda i, j: (i, j, 0)),
            blk(lambda i, j: (i, jnp.maximum(j - 1, 0), 0)),
            blk(lambda i, j: (i, j, 0)),
            blk(lambda i, j: (i, jnp.minimum(j + 1, nb - 1), 0)),
            pl.BlockSpec((1, lc, 2 * LANES), lambda i, j: (i, 0, 0)),
            pl.BlockSpec((1, ATT_BLOCK, ATT_WIDTH), lambda i, j: (i, j, 0)),
        ],
        out_specs=pl.BlockSpec((1, ATT_BLOCK, ATT_WIDTH), lambda i, j: (i, j, 0)),
        out_shape=jax.ShapeDtypeStruct((b, l, ATT_WIDTH), BF16),
        compiler_params=_cparams(("arbitrary", "arbitrary")),
        name="attn_local" if local else "attn_ctx",
    )(sink, q, kv, kv, kv, kvc, zb)


def _merge_kernel(x_ref, gate_ref, of_ref, ob_ref, za_ref, yb_ref, ga_ref, gb_ref, gnw_ref,
                  wpa_ref, wpb_ref, wout_ref, o_ref):
    gnw = gnw_ref[...]
    ya_parts = []
    for h in range(GDN_HEADS):
        sl = slice(h * LANES, (h + 1) * LANES)
        o = of_ref[0, :, sl].astype(F32) + ob_ref[0, :, sl].astype(F32)
        on = o * lax.rsqrt(jnp.mean(o * o, axis=-1, keepdims=True) + EPS) * gnw
        ya_parts.append((on * za_ref[0, :, sl].astype(F32)).astype(BF16))
    ya = jnp.concatenate(ya_parts, axis=1)
    pa = jnp.dot(ya, wpa_ref[...], preferred_element_type=F32)
    pb = jnp.dot(yb_ref[0], wpb_ref[...], preferred_element_type=F32)
    y = ga_ref[0].astype(F32) * pa + gb_ref[0].astype(F32) * pb
    out = jnp.dot(y.astype(BF16), wout_ref[...], preferred_element_type=F32)
    o_ref[0] = x_ref[0] + gate_ref[0] * out


def _merge_call(x, mod, mod_row, layer, of, ob, za, yb, ga, gb, gnw, wpa, wpb, wout, *, tm):
    b, l, _ = x.shape
    rows = mod.shape[0] // (3 * DEPTH)
    row = lambda w: pl.BlockSpec((1, tm, w), lambda i, j: (i, j, 0))
    return pl.pallas_call(
        _merge_kernel,
        grid=(b, l // tm),
        in_specs=[
            row(D_MODEL), _mod_spec(layer, 2, rows, mod_row),
            row(GDN_WIDTH), row(GDN_WIDTH), row(GDN_WIDTH), row(ATT_WIDTH), row(D_MODEL), row(D_MODEL),
            _layer_spec(layer, 1, LANES),
            _layer_spec(layer, GDN_WIDTH, D_MODEL),
            _layer_spec(layer, ATT_WIDTH, D_MODEL),
            _layer_spec(layer, D_MODEL, D_MODEL),
        ],
        out_specs=row(D_MODEL),
        out_shape=jax.ShapeDtypeStruct((b, l, D_MODEL), F32),
        compiler_params=_cparams(("arbitrary", "arbitrary")),
        name="merge",
    )(x, mod, of, ob, za, yb, ga, gb, gnw, wpa, wpb, wout)


def _pick_tile(l, pref):
    t = min(pref, l)
    while l % t:
        t //= 2
    return t


def kernel(x, c, ctx, c_ctx, norm_w, w_mod, b_mod, w_in, conv_w, a_log, dt_bias, gdn_norm_w,
           q_norm_w, k_norm_w, sink, w_proj_a, w_proj_b, w_out):
    b, l, _ = x.shape
    lc = ctx.shape[1]
    assert l % ATT_BLOCK == 0 and lc % ATT_BLOCK == 0 and x.shape[2] == D_MODEL

    rows = ((b + 1 + 7) // 8) * 8
    c_all = jnp.concatenate([c, c_ctx[None, :], jnp.zeros((rows - b - 1, D_MODEL), F32)], axis=0)
    mod = _mod_call(c_all, w_mod, b_mod).reshape(DEPTH * 3 * rows, 1, D_MODEL)

    cos_l, sin_l = _rope_tables(l)
    cos_c = jnp.ones((lc, LANES), F32)
    sin_c = jnp.zeros((lc, LANES), F32)
    grp = (np.arange(LANES) // ROPE_F) % ATT_KV_HEADS
    gmat = jnp.asarray((grp[:, None] == grp[None, :]).astype(np.float32), dtype=BF16)
    s0 = jnp.zeros((b, 2, GDN_HEADS, LANES, LANES), F32)
    tp_l = _pick_tile(l, 1024)
    tm_l = _pick_tile(l, 1024)
    tm_c = _pick_tile(lc, 256)
    tc_l = _pick_tile(l, 512)
    tc_c = _pick_tile(lc, 256)
    tch_l = _pick_tile(l, 1024)

    w_packed_all = _pack_w_in(w_in)
    ng = 2 * GDN_HEADS
    gpad = jnp.zeros((DEPTH, ng), F32)
    gvec_all = jnp.concatenate([
        jnp.concatenate([gpad, jnp.exp(a_log.astype(F32)).reshape(DEPTH, ng),
                         jnp.zeros((DEPTH, LANES - 2 * ng), F32)], axis=1)[:, None, :],
        jnp.concatenate([gpad, dt_bias.astype(F32).reshape(DEPTH, ng),
                         jnp.zeros((DEPTH, LANES - 2 * ng), F32)], axis=1)[:, None, :],
        jnp.zeros((DEPTH, 6, LANES), F32)], axis=1)
    qn_all = (jax.vmap(_norm_lanes)(q_norm_w) * (ATT_HD ** -0.5 * LOG2E))[:, None, :]
    kn_all = jax.vmap(_norm_lanes)(k_norm_w)[:, None, :]
    cw_all = jnp.pad(conv_w, ((0, 0), (0, 8 - SHORT_CONV), (0, 0)))
    normw_all = norm_w[:, None, :]
    gnw_all = gdn_norm_w[:, None, :]
    wpa_all = w_proj_a.astype(BF16)
    wpb_all = jnp.swapaxes(w_proj_b.reshape(DEPTH, ATT_KV_HEADS, ATT_GROUP, ATT_HD, D_MODEL), 1, 2).reshape(
        DEPTH, ATT_WIDTH, D_MODEL).astype(BF16)
    wout_all = w_out.astype(BF16)
    proj_params = (normw_all, w_packed_all, gvec_all, qn_all, kn_all, gmat)
    merge_params = (gnw_all, wpa_all, wpb_all, wout_all)

    for i in range(DEPTH):
        update_ctx = i < DEPTH - 1

        outs_c = _proj_call(ctx, mod, b, i, *proj_params, cos_c, sin_c, with_rest=update_ctx, tm=tm_c)
        qkv_c, gates_c, kv_c = outs_c[:3]
        prep_c = _prep_call(qkv_c, cw_all, i, gates_c, tc=tc_c)
        of_c, ob_c, s_ctx = _chain_call(*prep_c, s0, tc=tc_c)

        qkv_l, gates_l, kv_l, za_l, q_l, zb_l, ga_l, gb_l = _proj_call(
            x, mod, None, i, *proj_params, cos_l, sin_l, with_rest=True, tm=tp_l)
        prep_l = _prep_call(qkv_l, cw_all, i, gates_l, tc=tc_l)
        of_l, ob_l, _ = _chain_call(*prep_l, s_ctx, tc=tch_l)
        yb_l = _attn_call(sink, i, q_l, kv_l, kv_c, zb_l, local=True)
        x_new = _merge_call(x, mod, None, i, of_l, ob_l, za_l, yb_l, ga_l, gb_l, *merge_params, tm=tm_l)

        if update_ctx:
            za_c, q_c, zb_c, ga_c, gb_c = outs_c[3:]
            yb_c = _attn_call(sink, i, q_c, kv_c, kv_c, zb_c, local=False)
            ctx = _merge_call(ctx, mod, b, i, of_c, ob_c, za_c, yb_c, ga_c, gb_c, *merge_params, tm=tm_c)
        x = x_new
    return x
```
